```python
import jax, jax.numpy as jnp
from jax import lax
import numpy as np

D_MODEL = 4096
BATCH = 4
SEQ = 2048
DEPTH = 2
DEC_BATCH = 128
DEC_SEQ = 1
PAST_LEN = 16384
PAGE_SIZE = 128

PLE_DIM = 256
D_FF = 256 * ((8 * D_MODEL // 3 + 255) // 256)
POOL_W = D_MODEL // 4
POOL_WINDOWS = (2, 4, 8, 16)
POOL_GD = POOL_W // len(POOL_WINDOWS)
POOL_BUF = max(POOL_WINDOWS) - 1
SGU_W = D_MODEL // 4
SGU_GROUPS = 8
SGU_GD = SGU_W // SGU_GROUPS
SGU_CHUNK = 128
GLA_HEADS = 4
GLA_DK = D_MODEL // 4
GLA_DV = D_MODEL // 2
GLA_HK = GLA_DK // GLA_HEADS
GLA_HV = GLA_DV // GLA_HEADS
GLA_RANK = 16
GLA_TAU = 16.0
GLA_CHUNK = 64
IN_SIZES = (POOL_W, SGU_W, SGU_W, GLA_DK, GLA_DK, GLA_DV, GLA_DV, GLA_RANK, D_MODEL, D_MODEL, D_MODEL)
N_IN = sum(IN_SIZES)
ALPHA = (2.0 * DEPTH) ** 0.25
BETA = (8.0 * DEPTH) ** -0.25
EPS = 1e-5

kernel_name = "hybrid_pool_sgu_gla_decoder_step"


def layer_norm(x, g, b):
    xf = x.astype(jnp.float32)
    mu = jnp.mean(xf, -1, keepdims=True)
    var = jnp.mean(jnp.square(xf - mu), -1, keepdims=True)
    return ((xf - mu) * lax.rsqrt(var + EPS) * g.astype(jnp.float32) + b.astype(jnp.float32)).astype(x.dtype)


def swiglu(x, w1, w3, w2):
    return (jax.nn.silu(x @ w1) * (x @ w3)) @ w2


def in_split_points():
    return [int(s) for s in np.cumsum(IN_SIZES)[:-1]]


def pool_mix(a, buf, pos0, w_grp, scale):
    T = a.shape[1]
    ext = jnp.concatenate([buf.astype(a.dtype), a], axis=1)
    csum = jnp.pad(jnp.cumsum(ext.astype(jnp.float32), axis=1), ((0, 0), (1, 0), (0, 0)))
    end = csum[:, POOL_BUF + 1:POOL_BUF + 1 + T]
    pos = pos0 + jnp.arange(T)
    outs = []
    for gi, w in enumerate(POOL_WINDOWS):
        sl = slice(gi * POOL_GD, (gi + 1) * POOL_GD)
        start = csum[:, POOL_BUF + 1 - w:POOL_BUF + 1 - w + T, sl]
        cnt = jnp.minimum(pos + 1, w).astype(jnp.float32)[None, :, None]
        d = (end[..., sl] - start) / cnt - a[..., sl].astype(jnp.float32)
        outs.append(jnp.einsum('ntc,cd->ntd', d.astype(a.dtype), w_grp[gi]))
    return jnp.concatenate(outs, axis=-1) * scale, ext[:, -POOL_BUF:]


def sgu_mix(u, v, ln_g, ln_b, ws, bias):
    N, T, _ = v.shape
    vn = layer_norm(v, ln_g, ln_b)
    pad = (-T) % SGU_CHUNK
    nc = (T + pad) // SGU_CHUNK
    vc = jnp.pad(vn, ((0, 0), (0, pad), (0, 0))).reshape(N, nc, SGU_CHUNK, SGU_GROUPS, SGU_GD)
    w_m = ws * jnp.tril(jnp.ones((SGU_CHUNK, SGU_CHUNK), ws.dtype))[None]
    mixed = jnp.einsum('gij,ncjgd->ncigd', w_m, vc) + bias.T[None, None, :, :, None]
    mixed = mixed.reshape(N, nc * SGU_CHUNK, SGU_W)[:, :T]
    return u * mixed, vn


def gla_chunked(q, k, v, g, s0):
    N, T, H, _ = q.shape
    C = min(GLA_CHUNK, T)
    pad = (-T) % C
    n = (T + pad) // C

    def blocks(t):
        t = jnp.pad(t, ((0, 0), (0, pad), (0, 0), (0, 0)))
        return t.reshape(N, n, C, H, t.shape[-1]).transpose(1, 0, 3, 2, 4)

    causal = jnp.tril(jnp.ones((C, C), jnp.float32))

    def step(S, inp):
        qc, kc, vc, gc = inp
        b = jnp.cumsum(gc, axis=2)
        b_end = b[:, :, -1:, :]
        q_in = qc * jnp.exp(b)
        k_in = kc * jnp.exp(-b)
        att = jnp.einsum('nhtd,nhsd->nhts', q_in, k_in) * causal
        o = jnp.einsum('nhts,nhsv->nhtv', att, vc) + jnp.einsum('nhtd,nhdv->nhtv', q_in, S)
        k_dec = kc * jnp.exp(b_end - b)
        S = S * jnp.exp(b_end[:, :, 0, :, None]) + jnp.einsum('nhsd,nhsv->nhdv', k_dec, vc)
        return S, o

    s_T, o = lax.scan(step, s0, (blocks(q), blocks(k), blocks(v), blocks(g)))
    o = o.transpose(1, 0, 3, 2, 4).reshape(N, n * C, H, v.shape[-1])[:, :T]
    return o, s_T


def gla_mix(q, k, v, r, a_lr, s0, wa2, ba, norm_g):
    N, T, _ = q.shape
    f32 = jnp.float32
    logf = jax.nn.log_sigmoid((a_lr @ wa2 + ba).astype(f32)) / GLA_TAU
    hq = q.astype(f32).reshape(N, T, GLA_HEADS, GLA_HK) * GLA_HK ** -0.5
    hk = k.astype(f32).reshape(N, T, GLA_HEADS, GLA_HK)
    hv = v.astype(f32).reshape(N, T, GLA_HEADS, GLA_HV)
    o, s_new = gla_chunked(hq, hk, hv, logf.reshape(N, T, GLA_HEADS, GLA_HK), s0.astype(f32))
    o = o * lax.rsqrt(jnp.mean(jnp.square(o), -1, keepdims=True) + EPS) * norm_g.astype(f32).reshape(GLA_HEADS, GLA_HV)
    return o.reshape(N, T, GLA_DV).astype(q.dtype) * jax.nn.silu(r), s_new


def decoder_layer(x, p, pool_buf, gla_s, pos0, lw):
    ln_g, ln_b = lw['ln_g'], lw['ln_b']
    x = layer_norm(ALPHA * x + 0.5 * swiglu(x, lw['ffa_w1'], lw['ffa_w3'], lw['ffa_w2']), ln_g[0], ln_b[0])
    z = x @ lw['w_in'] + lw['b_in']
    a_pool, u, v, q, k, vv, r, a_lr, g_a, g_b, g_c = jnp.split(z, in_split_points(), axis=-1)
    ya, new_buf = pool_mix(a_pool, pool_buf, pos0, lw['pool_w'], lw['pool_scale'])
    yb, v_rows = sgu_mix(u, v, lw['sgu_ln_g'], lw['sgu_ln_b'], lw['sgu_ws'], lw['sgu_bias'])
    yc, new_s = gla_mix(q, k, vv, r, a_lr, gla_s, lw['gla_wa2'], lw['gla_ba'], lw['gla_norm_g'])
    h = (jax.nn.sigmoid(g_a) * (ya @ lw['w_up_a'])
         + jax.nn.sigmoid(g_b) * (yb @ lw['w_up_b'])
         + jax.nn.sigmoid(g_c) * (yc @ lw['w_up_c']))
    x = layer_norm(ALPHA * x + h @ lw['w_o'], ln_g[1], ln_b[1])
    x = layer_norm(ALPHA * x + 0.5 * swiglu(x, lw['ffb_w1'], lw['ffb_w3'], lw['ffb_w2']), ln_g[2], ln_b[2])
    x = layer_norm(ALPHA * x + jax.nn.sigmoid(x @ lw['pe_gate_w']) * (p @ lw['pe_w']), ln_g[3], ln_b[3])
    return x, new_buf, new_s, v_rows


def setup_inputs(seed: int = 0) -> dict:
    key = jax.random.key(seed)
    ks = jax.random.split(key, 40)
    f32 = jnp.float32
    L = DEPTH

    def nrm(i, shape, scale=1.0):
        return jax.random.normal(ks[i], shape, f32) * scale

    return {
        'x_prompt': nrm(0, (BATCH, SEQ, D_MODEL)),
        'x_sample': nrm(1, (DEC_BATCH, DEC_SEQ, D_MODEL)),
        'state_pool': nrm(2, (L, DEC_BATCH, POOL_BUF, POOL_W)),
        'state_gla': nrm(3, (L, DEC_BATCH, GLA_HEADS, GLA_HK, GLA_HV)),
        'p_prompt': nrm(4, (L, BATCH, SEQ, PLE_DIM)),
        'p_sample': nrm(5, (L, DEC_BATCH, DEC_SEQ, PLE_DIM)),
        'ffa_w1': nrm(6, (L, D_MODEL, D_FF), BETA * D_MODEL ** -0.5),
        'ffa_w3': nrm(7, (L, D_MODEL, D_FF), BETA * D_MODEL ** -0.5),
        'ffa_w2': nrm(8, (L, D_FF, D_MODEL), BETA * D_FF ** -0.5),
        'ffb_w1': nrm(9, (L, D_MODEL, D_FF), BETA * D_MODEL ** -0.5),
        'ffb_w3': nrm(10, (L, D_MODEL, D_FF), BETA * D_MODEL ** -0.5),
        'ffb_w2': nrm(11, (L, D_FF, D_MODEL), BETA * D_FF ** -0.5),
        'ln_g': 1.0 + nrm(12, (L, 4, D_MODEL), 0.05),
        'ln_b': nrm(13, (L, 4, D_MODEL), 0.05),
        'w_in': nrm(14, (L, D_MODEL, N_IN), D_MODEL ** -0.5),
        'b_in': nrm(15, (L, N_IN), 0.02),
        'pool_w': nrm(16, (L, len(POOL_WINDOWS), POOL_GD, POOL_GD), POOL_GD ** -0.5),
        'pool_scale': 1.0 + nrm(17, (L, POOL_W), 0.1),
        'sgu_ln_g': 1.0 + nrm(18, (L, SGU_W), 0.05),
        'sgu_ln_b': nrm(19, (L, SGU_W), 0.05),
        'sgu_ws': nrm(20, (L, SGU_GROUPS, SGU_CHUNK, SGU_CHUNK), SGU_CHUNK ** -0.5),
        'sgu_bias': 1.0 + nrm(21, (L, SGU_GROUPS, SGU_CHUNK), 0.1),
        'gla_wa2': nrm(22, (L, GLA_RANK, GLA_DK), GLA_RANK ** -0.5),
        'gla_ba': nrm(23, (L, GLA_DK), 0.1),
        'gla_norm_g': 1.0 + nrm(24, (L, GLA_DV), 0.05),
        'w_up_a': nrm(25, (L, POOL_W, D_MODEL), BETA * POOL_W ** -0.5),
        'w_up_b': nrm(26, (L, SGU_W, D_MODEL), BETA * SGU_W ** -0.5),
        'w_up_c': nrm(27, (L, GLA_DV, D_MODEL), BETA * GLA_DV ** -0.5),
        'w_o': nrm(28, (L, D_MODEL, D_MODEL), BETA * D_MODEL ** -0.5),
        'pe_w': nrm(29, (L, PLE_DIM, D_MODEL), BETA * PLE_DIM ** -0.5),
        'pe_gate_w': nrm(30, (L, D_MODEL, D_MODEL), D_MODEL ** -0.5),
    }


def reference(x_prompt, x_sample, state_pool, state_gla, p_prompt, p_sample,
              ffa_w1, ffa_w3, ffa_w2, ffb_w1, ffb_w3, ffb_w2, ln_g, ln_b,
              w_in, b_in, pool_w, pool_scale, sgu_ln_g, sgu_ln_b, sgu_ws, sgu_bias,
              gla_wa2, gla_ba, gla_norm_g, w_up_a, w_up_b, w_up_c, w_o, pe_w, pe_gate_w):
    hp, hs = x_prompt, x_sample
    pool_p, gla_p, pool_s, gla_s, sgu_s = [], [], [], [], []
    for i in range(DEPTH):
        lw = dict(ffa_w1=ffa_w1[i], ffa_w3=ffa_w3[i], ffa_w2=ffa_w2[i],
                  ffb_w1=ffb_w1[i], ffb_w3=ffb_w3[i], ffb_w2=ffb_w2[i],
                  ln_g=ln_g[i], ln_b=ln_b[i], w_in=w_in[i], b_in=b_in[i],
                  pool_w=pool_w[i], pool_scale=pool_scale[i],
                  sgu_ln_g=sgu_ln_g[i], sgu_ln_b=sgu_ln_b[i], sgu_ws=sgu_ws[i], sgu_bias=sgu_bias[i],
                  gla_wa2=gla_wa2[i], gla_ba=gla_ba[i], gla_norm_g=gla_norm_g[i],
                  w_up_a=w_up_a[i], w_up_b=w_up_b[i], w_up_c=w_up_c[i], w_o=w_o[i],
                  pe_w=pe_w[i], pe_gate_w=pe_gate_w[i])
        nb = hp.shape[0]
        zero_buf = jnp.zeros((nb, POOL_BUF, POOL_W), hp.dtype)
        zero_s = jnp.zeros((nb, GLA_HEADS, GLA_HK, GLA_HV), jnp.float32)
        hp, bp, sp, _ = decoder_layer(hp, p_prompt[i], zero_buf, zero_s, 0, lw)
        hs, bs, ss, vs = decoder_layer(hs, p_sample[i], state_pool[i], state_gla[i], PAST_LEN, lw)
        pool_p.append(bp)
        gla_p.append(sp.astype(state_gla.dtype))
        pool_s.append(bs)
        gla_s.append(ss.astype(state_gla.dtype))
        sgu_s.append(vs)
    return (hp, hs, jnp.stack(pool_p), jnp.stack(gla_p), jnp.stack(pool_s), jnp.stack(gla_s), jnp.stack(sgu_s))
```

```python
import functools

import jax
import jax.numpy as jnp
from jax import lax
from jax.experimental import pallas as pl
from jax.experimental.pallas import tpu as pltpu

F32 = jnp.float32
BF16 = jnp.bfloat16

POOL_WINDOWS = (2, 4, 8, 16)
POOL_BUF = max(POOL_WINDOWS) - 1
POOL_HALO = 16
SGU_GROUPS = 8
SGU_CHUNK = 128
GLA_HEADS = 4
GLA_RANK = 16
GLA_TAU = 16.0
GLA_CHUNK = 64
EPS = 1e-5
SAMPLE_BLOCK = 8

V7X_VMEM_LIMIT_BYTES = 60 * 1024 * 1024


def _tile(dim, target, align):
    best = None
    t = align
    while t <= min(dim, target):
        if dim % t == 0:
            best = t
        t += align
    return dim if best is None else best


def _params(*sem):
    return pltpu.CompilerParams(dimension_semantics=sem, vmem_limit_bytes=V7X_VMEM_LIMIT_BYTES)


def _silu(x):
    return x * jax.nn.sigmoid(x)


def _dot(a, b):
    return jnp.dot(a, b, preferred_element_type=F32)


def _dot_nt(a, b):
    return lax.dot_general(a, b, (((1,), (1,)), ((), ())), preferred_element_type=F32)


def _dot_tn(a, b):
    return lax.dot_general(a, b, (((0,), (0,)), ((), ())), preferred_element_type=F32)


def _round_bf16(x):
    return x.astype(BF16).astype(F32)


def _glu_kernel(x_ref, w1_ref, w3_ref, o_ref):
    x = x_ref[...]
    a = _dot(x, w1_ref[...])
    b = _dot(x, w3_ref[...])
    o_ref[...] = (_silu(a) * b).astype(o_ref.dtype)


def _ffn_up(xb, w1, w3):
    m, k = xb.shape
    n = w1.shape[1]
    tm = _tile(m, 1040, 16)
    tn = _tile(n, 256, 128)
    return pl.pallas_call(
        _glu_kernel,
        grid=(m // tm, n // tn),
        in_specs=[pl.BlockSpec((tm, k), lambda i, j: (i, 0)),
                  pl.BlockSpec((k, tn), lambda i, j: (0, j)),
                  pl.BlockSpec((k, tn), lambda i, j: (0, j))],
        out_specs=pl.BlockSpec((tm, tn), lambda i, j: (i, j)),
        out_shape=jax.ShapeDtypeStruct((m, n), BF16),
        compiler_params=_params("parallel", "arbitrary"),
        name="ffn_up",
    )(xb, w1, w3)


def _mm_kernel(x_ref, w_ref, o_ref):
    o_ref[...] = _dot(x_ref[...], w_ref[...]).astype(o_ref.dtype)


def _mm_bias_kernel(x_ref, w_ref, b_ref, o_ref):
    o_ref[...] = (_dot(x_ref[...], w_ref[...]) + b_ref[...]).astype(o_ref.dtype)


def _matmul(xb, w, *, n, tm_target, tn_target, name, bias=None):
    m, k = xb.shape
    tm = _tile(m, tm_target, 16)
    tn = _tile(n, tn_target, 128)
    in_specs = [pl.BlockSpec((tm, k), lambda i, j: (i, 0)),
                pl.BlockSpec((k, tn), lambda i, j: (0, j))]
    args = [xb, w]
    kern = _mm_kernel
    if bias is not None:
        in_specs.append(pl.BlockSpec((1, tn), lambda i, j: (0, j)))
        args.append(bias)
        kern = _mm_bias_kernel
    return pl.pallas_call(
        kern,
        grid=(m // tm, n // tn),
        in_specs=in_specs,
        out_specs=pl.BlockSpec((tm, tn), lambda i, j: (i, j)),
        out_shape=jax.ShapeDtypeStruct((m, n), F32),
        compiler_params=_params("parallel", "arbitrary"),
        name=name,
    )(*args)


def _ln_kernel(x_ref, y_ref, g_ref, b_ref, of_ref, ob_ref, *, alpha, yscale):
    y = y_ref[...]
    if yscale is not None:
        y = yscale * y
    t = alpha * x_ref[...] + y
    mu = jnp.mean(t, -1, keepdims=True)
    d = t - mu
    var = jnp.mean(d * d, -1, keepdims=True)
    o = d * lax.rsqrt(var + EPS) * g_ref[...] + b_ref[...]
    of_ref[...] = o
    ob_ref[...] = o.astype(BF16)


def _res_ln(x, y, g, b, *, alpha, yscale):
    m, d = x.shape
    tm = _tile(m, 208, 16)
    row = pl.BlockSpec((tm, d), lambda i: (i, 0))
    vec = pl.BlockSpec((1, d), lambda i: (0, 0))
    return pl.pallas_call(
        functools.partial(_ln_kernel, alpha=alpha, yscale=yscale),
        grid=(m // tm,),
        in_specs=[row, row, vec, vec],
        out_specs=[row, row],
        out_shape=[jax.ShapeDtypeStruct((m, d), F32), jax.ShapeDtypeStruct((m, d), BF16)],
        compiler_params=_params("parallel"),
        name="res_ln",
    )(x, y, g.reshape(1, d), b.reshape(1, d))


def _log_sigmoid(x):
    return jnp.minimum(x, 0.0) - jnp.log1p(jnp.exp(-jnp.abs(x)))


def _logf_kernel(x_ref, wl_ref, bl_ref, wa_ref, ba_ref, o_ref):
    a_lr = _dot(x_ref[...], wl_ref[...]) + bl_ref[...]
    t = _dot(a_lr.astype(BF16), wa_ref[...]) + ba_ref[...]
    o_ref[...] = _log_sigmoid(t) / GLA_TAU


def _gla_logf(xb, w_lr, b_lr, wa2, ba):
    m, k = xb.shape
    rp = w_lr.shape[1]
    dk = wa2.shape[1]
    tm = _tile(m, 1040, 16)
    return pl.pallas_call(
        _logf_kernel,
        grid=(m // tm,),
        in_specs=[pl.BlockSpec((tm, k), lambda i: (i, 0)),
                  pl.BlockSpec((k, rp), lambda i: (0, 0)),
                  pl.BlockSpec((1, rp), lambda i: (0, 0)),
                  pl.BlockSpec((rp, dk), lambda i: (0, 0)),
                  pl.BlockSpec((1, dk), lambda i: (0, 0))],
        out_specs=pl.BlockSpec((tm, dk), lambda i: (i, 0)),
        out_shape=jax.ShapeDtypeStruct((m, dk), F32),
        compiler_params=_params("parallel"),
        name="gla_logf",
    )(xb, w_lr, b_lr, wa2, ba)


def _upgate_kernel(x_ref, ya_ref, yb_ref, yc_ref, wga_ref, wgb_ref, wgc_ref, bga_ref, bgb_ref, bgc_ref,
                   wa_ref, wb_ref, wc_ref, o_ref):
    x = x_ref[...]
    acc = jax.nn.sigmoid(_dot(x, wga_ref[...]) + bga_ref[...]) * _dot(ya_ref[...], wa_ref[...])
    acc = acc + jax.nn.sigmoid(_dot(x, wgb_ref[...]) + bgb_ref[...]) * _dot(yb_ref[...], wb_ref[...])
    acc = acc + jax.nn.sigmoid(_dot(x, wgc_ref[...]) + bgc_ref[...]) * _dot(yc_ref[...], wc_ref[...])
    o_ref[...] = acc.astype(o_ref.dtype)


def _up_gate(xb, ya, yb, yc, w_g, b_g, w_up_a, w_up_b, w_up_c):
    m, k = xb.shape
    d = w_up_a.shape[1]
    tm = _tile(m, 832, 16)
    tn = _tile(d, 256, 128)
    nb = d // tn

    def rows(width):
        return pl.BlockSpec((tm, width), lambda i, j: (i, 0))

    def gate_cols(branch):
        return pl.BlockSpec((k, tn), lambda i, j: (0, branch * nb + j))

    def gate_bias(branch):
        return pl.BlockSpec((1, tn), lambda i, j: (0, branch * nb + j))

    def up_cols(width):
        return pl.BlockSpec((width, tn), lambda i, j: (0, j))

    return pl.pallas_call(
        _upgate_kernel,
        grid=(m // tm, nb),
        in_specs=[rows(k), rows(ya.shape[1]), rows(yb.shape[1]), rows(yc.shape[1]),
                  gate_cols(0), gate_cols(1), gate_cols(2),
                  gate_bias(0), gate_bias(1), gate_bias(2),
                  up_cols(ya.shape[1]), up_cols(yb.shape[1]), up_cols(yc.shape[1])],
        out_specs=pl.BlockSpec((tm, tn), lambda i, j: (i, j)),
        out_shape=jax.ShapeDtypeStruct((m, d), BF16),
        compiler_params=_params("parallel", "arbitrary"),
        name="up_gate",
    )(xb, ya, yb, yc, w_g, w_g, w_g, b_g, b_g, b_g, w_up_a, w_up_b, w_up_c)


def _pe_kernel(x_ref, p_ref, wg_ref, wp_ref, o_ref):
    o_ref[...] = jax.nn.sigmoid(_dot(x_ref[...], wg_ref[...])) * _dot(p_ref[...], wp_ref[...])


def _pe_embed(xb, pb, w_gate, w_p):
    m, k = xb.shape
    d = w_gate.shape[1]
    kp = pb.shape[1]
    tm = _tile(m, 1040, 16)
    tn = _tile(d, 512, 128)
    return pl.pallas_call(
        _pe_kernel,
        grid=(m // tm, d // tn),
        in_specs=[pl.BlockSpec((tm, k), lambda i, j: (i, 0)),
                  pl.BlockSpec((tm, kp), lambda i, j: (i, 0)),
                  pl.BlockSpec((k, tn), lambda i, j: (0, j)),
                  pl.BlockSpec((kp, tn), lambda i, j: (0, j))],
        out_specs=pl.BlockSpec((tm, tn), lambda i, j: (i, j)),
        out_shape=jax.ShapeDtypeStruct((m, d), F32),
        compiler_params=_params("parallel", "arbitrary"),
        name="pe_embed",
    )(xb, pb, w_gate, w_p)


def _pool_prompt_kernel(a_ref, pw_ref, ps_ref, o_ref, ext_ref, *, tt, gd):
    t = pl.program_id(1)

    @pl.when(t == 0)
    def _():
        ext_ref[0:POOL_HALO, :] = jnp.zeros((POOL_HALO, ext_ref.shape[1]), F32)

    a = a_ref[...]
    ext_ref[POOL_HALO:POOL_HALO + tt, :] = a
    pos = t * tt + lax.broadcasted_iota(jnp.int32, (tt, 1), 0)
    for gi, w in enumerate(POOL_WINDOWS):
        sl = slice(gi * gd, (gi + 1) * gd)
        acc = a[:, sl]
        for s in range(1, w):
            acc = acc + ext_ref[POOL_HALO - s:POOL_HALO - s + tt, sl]
        cnt = jnp.minimum(pos + 1, w).astype(F32)
        d = acc / cnt - a[:, sl]
        y = _dot(d.astype(BF16), pw_ref[gi])
        o_ref[:, sl] = (y * ps_ref[:, sl]).astype(o_ref.dtype)
    ext_ref[0:POOL_HALO, :] = ext_ref[tt:tt + POOL_HALO, :]


def _pool_prompt(z, pool_w, pool_scale, *, batch, seq, m, width):
    tt = _tile(seq, 256, 16)
    nt = seq // tt
    gd = width // len(POOL_WINDOWS)
    return pl.pallas_call(
        functools.partial(_pool_prompt_kernel, tt=tt, gd=gd),
        grid=(batch, nt),
        in_specs=[pl.BlockSpec((tt, width), lambda b, t: (b * nt + t, 0)),
                  pl.BlockSpec(pool_w.shape, lambda b, t: (0, 0, 0)),
                  pl.BlockSpec((1, width), lambda b, t: (0, 0))],
        out_specs=pl.BlockSpec((tt, width), lambda b, t: (b * nt + t, 0)),
        out_shape=jax.ShapeDtypeStruct((m, width), BF16),
        scratch_shapes=[pltpu.VMEM((tt + POOL_HALO, width), F32)],
        compiler_params=_params("parallel", "arbitrary"),
        name="pool_prompt",
    )(z, pool_w, pool_scale)


def _pool_sample_kernel(a_ref, buf_ref, pw_ref, ps_ref, ya_in_ref, o_ref, nbuf_ref, *, width, gd):
    del ya_in_ref
    a = a_ref[...]
    for gi, w in enumerate(POOL_WINDOWS):
        sl = slice(gi * gd, (gi + 1) * gd)
        acc = a[:, sl]
        for s in range(1, w):
            r = POOL_BUF - s
            acc = acc + buf_ref[:, r * width + gi * gd:r * width + (gi + 1) * gd]
        d = acc / float(w) - a[:, sl]
        y = _dot(d.astype(BF16), pw_ref[gi])
        o_ref[:, sl] = (y * ps_ref[:, sl]).astype(o_ref.dtype)
    nbuf_ref[:, 0:(POOL_BUF - 1) * width] = buf_ref[:, width:POOL_BUF * width]
    nbuf_ref[:, (POOL_BUF - 1) * width:POOL_BUF * width] = a


def _pool_sample(z, buf_flat, pool_w, pool_scale, ya, *, row0, nsamp, width):
    ns = SAMPLE_BLOCK
    gd = width // len(POOL_WINDOWS)
    rb = row0 // ns
    ya, nbuf = pl.pallas_call(
        functools.partial(_pool_sample_kernel, width=width, gd=gd),
        grid=(nsamp // ns,),
        in_specs=[pl.BlockSpec((ns, width), lambda i: (rb + i, 0)),
                  pl.BlockSpec((ns, POOL_BUF * width), lambda i: (i, 0)),
                  pl.BlockSpec(pool_w.shape, lambda i: (0, 0, 0)),
                  pl.BlockSpec((1, width), lambda i: (0, 0)),
                  pl.BlockSpec(memory_space=pl.ANY)],
        out_specs=[pl.BlockSpec((ns, width), lambda i: (rb + i, 0)),
                   pl.BlockSpec((ns, POOL_BUF * width), lambda i: (i, 0))],
        out_shape=[jax.ShapeDtypeStruct(ya.shape, ya.dtype),
                   jax.ShapeDtypeStruct(buf_flat.shape, F32)],
        input_output_aliases={4: 0},
        compiler_params=_params("parallel"),
        name="pool_sample",
    )(z, buf_flat, pool_w, pool_scale, ya)
    return ya, nbuf


def _row_layer_norm(v, g, b):
    mu = jnp.mean(v, -1, keepdims=True)
    d = v - mu
    var = jnp.mean(d * d, -1, keepdims=True)
    return d * lax.rsqrt(var + EPS) * g + b


def _sgu_prompt_kernel(u_ref, v_ref, lng_ref, lnb_ref, ws_ref, biast_ref, o_ref, *, nchunk, gd):
    c = SGU_CHUNK
    tril = (lax.broadcasted_iota(jnp.int32, (c, c), 0) >= lax.broadcasted_iota(jnp.int32, (c, c), 1)).astype(F32)
    w_m = [(ws_ref[g] * tril).astype(BF16) for g in range(SGU_GROUPS)]
    for ci in range(nchunk):
        rows = slice(ci * c, (ci + 1) * c)
        vn = _row_layer_norm(v_ref[rows, :], lng_ref[...], lnb_ref[...]).astype(BF16)
        for g in range(SGU_GROUPS):
            sl = slice(g * gd, (g + 1) * gd)
            mixed = _dot(w_m[g], vn[:, sl]) + biast_ref[:, g:g + 1]
            o_ref[rows, sl] = (u_ref[rows, sl] * mixed).astype(o_ref.dtype)


def _sgu_prompt(z, ln_g, ln_b, ws, bias_t, *, batch, seq, m, width, u_blk, v_blk):
    assert seq % SGU_CHUNK == 0
    rows = _tile(seq, 512, SGU_CHUNK)
    nt = seq // rows
    gd = width // SGU_GROUPS
    return pl.pallas_call(
        functools.partial(_sgu_prompt_kernel, nchunk=rows // SGU_CHUNK, gd=gd),
        grid=(batch * nt,),
        in_specs=[pl.BlockSpec((rows, width), lambda i: (i, u_blk)),
                  pl.BlockSpec((rows, width), lambda i: (i, v_blk)),
                  pl.BlockSpec((1, width), lambda i: (0, 0)),
                  pl.BlockSpec((1, width), lambda i: (0, 0)),
                  pl.BlockSpec(ws.shape, lambda i: (0, 0, 0)),
                  pl.BlockSpec(bias_t.shape, lambda i: (0, 0))],
        out_specs=pl.BlockSpec((rows, width), lambda i: (i, 0)),
        out_shape=jax.ShapeDtypeStruct((m, width), BF16),
        compiler_params=_params("parallel"),
        name="sgu_prompt",
    )(z, z, ln_g, ln_b, ws, bias_t)


def _sgu_sample_kernel(u_ref, v_ref, lng_ref, lnb_ref, w0_ref, b0_ref, yb_in_ref, o_ref, vn_ref):
    del yb_in_ref
    vn = _row_layer_norm(v_ref[...], lng_ref[...], lnb_ref[...])
    vn_ref[...] = vn
    mixed = _round_bf16(w0_ref[...]) * _round_bf16(vn) + b0_ref[...]
    o_ref[...] = (u_ref[...] * mixed).astype(o_ref.dtype)


def _sgu_sample(z, ln_g, ln_b, w0, b0, yb, *, row0, nsamp, width, u_blk, v_blk):
    ns = SAMPLE_BLOCK
    rb = row0 // ns
    vec = pl.BlockSpec((1, width), lambda i: (0, 0))
    yb, vn = pl.pallas_call(
        _sgu_sample_kernel,
        grid=(nsamp // ns,),
        in_specs=[pl.BlockSpec((ns, width), lambda i: (rb + i, u_blk)),
                  pl.BlockSpec((ns, width), lambda i: (rb + i, v_blk)),
                  vec, vec, vec, vec,
                  pl.BlockSpec(memory_space=pl.ANY)],
        out_specs=[pl.BlockSpec((ns, width), lambda i: (rb + i, 0)),
                   pl.BlockSpec((ns, width), lambda i: (i, 0))],
        out_shape=[jax.ShapeDtypeStruct(yb.shape, yb.dtype),
                   jax.ShapeDtypeStruct((nsamp, width), F32)],
        input_output_aliases={6: 0},
        compiler_params=_params("parallel"),
        name="sgu_sample",
    )(z, z, ln_g, ln_b, w0, b0, yb)
    return yb, vn


def _cumsum_rows(x):
    rows = x.shape[0]
    row = lax.broadcasted_iota(jnp.int32, x.shape, 0)
    s = 1
    while s < rows:
        x = x + jnp.where(row >= s, pltpu.roll(x, s, 0), 0.0)
        s *= 2
    return x


def _gla_out(o, norm_g, r):
    o = o * lax.rsqrt(jnp.mean(o * o, -1, keepdims=True) + EPS) * norm_g
    return o * _silu(r)


def _gla_prompt_kernel(q_ref, k_ref, v_ref, g_ref, r_ref, ng_ref, o_ref, s_ref, st_ref, *, nchunk, scale):
    t = pl.program_id(2)
    c = GLA_CHUNK

    @pl.when(t == 0)
    def _():
        st_ref[...] = jnp.zeros(st_ref.shape, F32)

    causal = (lax.broadcasted_iota(jnp.int32, (c, c), 0) >= lax.broadcasted_iota(jnp.int32, (c, c), 1)).astype(F32)
    for ci in range(nchunk):
        rows = slice(ci * c, (ci + 1) * c)
        b = _cumsum_rows(g_ref[rows, :])
        b_end = b[c - 1:c, :]
        k = k_ref[rows, :]
        q_in = ((q_ref[rows, :] * scale) * jnp.exp(b)).astype(BF16)
        k_in = (k * jnp.exp(-b)).astype(BF16)
        k_dec = (k * jnp.exp(b_end - b)).astype(BF16)
        vb = v_ref[rows, :].astype(BF16)
        att = _dot_nt(q_in, k_in) * causal
        st = st_ref[...]
        o = _dot(att.astype(BF16), vb) + _dot_nt(q_in, st.astype(BF16))
        st_ref[...] = st * jnp.exp(b_end) + _dot_tn(vb, k_dec)
        o_ref[rows, :] = _gla_out(o, ng_ref[...], r_ref[rows, :]).astype(o_ref.dtype)

    @pl.when(t == pl.num_programs(2) - 1)
    def _():
        s_ref[0, 0] = st_ref[...].T


def _gla_prompt(z, logf, norm_g, *, batch, seq, m, hk, hv, q_blk, k_blk, v_blk, r_blk):
    assert seq % GLA_CHUNK == 0
    rows = _tile(seq, 256, GLA_CHUNK)
    nt = seq // rows
    heads = GLA_HEADS
    yc, s = pl.pallas_call(
        functools.partial(_gla_prompt_kernel, nchunk=rows // GLA_CHUNK, scale=hk ** -0.5),
        grid=(batch, heads, nt),
        in_specs=[pl.BlockSpec((rows, hk), lambda n, h, t: (n * nt + t, q_blk + h)),
                  pl.BlockSpec((rows, hk), lambda n, h, t: (n * nt + t, k_blk + h)),
                  pl.BlockSpec((rows, hv), lambda n, h, t: (n * nt + t, v_blk + h)),
                  pl.BlockSpec((rows, hk), lambda n, h, t: (n * nt + t, h)),
                  pl.BlockSpec((rows, hv), lambda n, h, t: (n * nt + t, r_blk + h)),
                  pl.BlockSpec((1, hv), lambda n, h, t: (0, h))],
        out_specs=[pl.BlockSpec((rows, hv), lambda n, h, t: (n * nt + t, h)),
                   pl.BlockSpec((1, 1, hk, hv), lambda n, h, t: (n, h, 0, 0))],
        out_shape=[jax.ShapeDtypeStruct((m, heads * hv), BF16),
                   jax.ShapeDtypeStruct((batch, heads, hk, hv), F32)],
        scratch_shapes=[pltpu.VMEM((hv, hk), F32)],
        compiler_params=_params("parallel", "parallel", "arbitrary"),
        name="gla_prompt",
    )(z, z, z, logf, z, norm_g)
    return yc, s


def _gla_sample_kernel(*refs, scale, has_prev):
    q_ref, k_ref, v_ref, g_ref, r_ref, ng_ref, s_ref, yc_in_ref = refs[:8]
    o_ref, so_ref = refs[-2:]
    del yc_in_ref
    ns = SAMPLE_BLOCK
    g = g_ref[...]
    e = jnp.exp(g)
    k = k_ref[...]
    q_in = _round_bf16((q_ref[...] * scale) * e)
    k_in = _round_bf16(k * jnp.exp(-g))
    k_dec = _round_bf16(k * jnp.exp(g - g))
    vb = _round_bf16(v_ref[...])
    att = jnp.sum(q_in * k_in, -1, keepdims=True)
    hk = g.shape[1]
    cols = jnp.concatenate([e, k_dec, q_in, jnp.zeros((128 - 3 * ns, hk), F32)], axis=0).T
    outs = []
    for j in range(ns):
        s = s_ref[0, j, 0]
        so_ref[0, j, 0] = s * cols[:, j:j + 1] + cols[:, ns + j:ns + j + 1] * vb[j:j + 1, :]
        outs.append(jnp.sum(cols[:, 2 * ns + j:2 * ns + j + 1] * _round_bf16(s), axis=0, keepdims=True))
    o = jnp.concatenate(outs, axis=0) + _round_bf16(att) * vb
    o_ref[...] = _gla_out(o, ng_ref[...], r_ref[...]).astype(o_ref.dtype)


def _gla_sample(z, logf, norm_g, state, layer, yc, prev_out, *, row0, nsamp, hk, hv, q_blk, k_blk, v_blk, r_blk):
    ns = SAMPLE_BLOCK
    rb = row0 // ns
    heads = GLA_HEADS
    st_spec = pl.BlockSpec((1, ns, 1, hk, hv), lambda i, h: (layer, i, h, 0, 0))
    in_specs = [pl.BlockSpec((ns, hk), lambda i, h: (rb + i, q_blk + h)),
                pl.BlockSpec((ns, hk), lambda i, h: (rb + i, k_blk + h)),
                pl.BlockSpec((ns, hv), lambda i, h: (rb + i, v_blk + h)),
                pl.BlockSpec((ns, hk), lambda i, h: (rb + i, h)),
                pl.BlockSpec((ns, hv), lambda i, h: (rb + i, r_blk + h)),
                pl.BlockSpec((1, hv), lambda i, h: (0, h)),
                st_spec,
                pl.BlockSpec(memory_space=pl.ANY)]
    args = [z, z, z, logf, z, norm_g, state, yc]
    aliases = {7: 0}
    if prev_out is not None:
        in_specs.append(pl.BlockSpec(memory_space=pl.ANY))
        args.append(prev_out)
        aliases[8] = 1
    yc, s_out = pl.pallas_call(
        functools.partial(_gla_sample_kernel, scale=hk ** -0.5, has_prev=prev_out is not None),
        grid=(nsamp // ns, heads),
        in_specs=in_specs,
        out_specs=[pl.BlockSpec((ns, hv), lambda i, h: (rb + i, h)), st_spec],
        out_shape=[jax.ShapeDtypeStruct(yc.shape, yc.dtype),
                   jax.ShapeDtypeStruct(state.shape, F32)],
        input_output_aliases=aliases,
        compiler_params=_params("parallel", "arbitrary"),
        name="gla_sample",
    )(*args)
    return yc, s_out


def _ffn(x, xb, w1, w3, w2, g, b, alpha):
    h = _ffn_up(xb, w1, w3)
    y = _matmul(h, w2, n=w2.shape[1], tm_target=640, tn_target=512, name="ffn_down")
    return _res_ln(x, y, g, b, alpha=alpha, yscale=0.5)


def kernel(x_prompt, x_sample, state_pool, state_gla, p_prompt, p_sample, ffa_w1, ffa_w3, ffa_w2, ffb_w1, ffb_w3, ffb_w2, ln_g, ln_b, w_in, b_in, pool_w, pool_scale, sgu_ln_g, sgu_ln_b, sgu_ws, sgu_bias, gla_wa2, gla_ba, gla_norm_g, w_up_a, w_up_b, w_up_c, w_o, pe_w, pe_gate_w):
    depth = ffa_w1.shape[0]
    batch, seq, d = x_prompt.shape
    nsamp, dec_seq, _ = x_sample.shape
    assert dec_seq == 1 and nsamp % SAMPLE_BLOCK == 0
    mp = batch * seq
    m = mp + nsamp
    assert mp % SAMPLE_BLOCK == 0
    alpha = (2.0 * depth) ** 0.25

    pool_wd = pool_w.shape[1] * pool_w.shape[2]
    sgu_wd = sgu_ln_g.shape[1]
    dk = gla_wa2.shape[2]
    dv = gla_norm_g.shape[1]
    hk, hv = dk // GLA_HEADS, dv // GLA_HEADS
    n_main = pool_wd + 2 * sgu_wd + 2 * dk + 2 * dv
    n_gate = n_main + GLA_RANK
    assert w_in.shape[2] == n_gate + 3 * d
    assert pool_wd == sgu_wd
    u_blk, v_blk = pool_wd // sgu_wd, pool_wd // sgu_wd + 1
    q_off = pool_wd + 2 * sgu_wd
    q_blk, k_blk = q_off // hk, (q_off + dk) // hk
    v2_blk, r_blk = (q_off + 2 * dk) // hv, (q_off + 2 * dk + dv) // hv

    x = jnp.concatenate([x_prompt.reshape(mp, d), x_sample.reshape(nsamp, d)], axis=0)
    xb = x.astype(BF16)

    pool_p, gla_p, pool_s, sgu_s = [], [], [], []
    gla_s = None
    rank_pad = 128 - GLA_RANK
    for i in range(depth):
        bf = lambda w: w[i].astype(BF16)
        x, xb = _ffn(x, xb, bf(ffa_w1), bf(ffa_w3), bf(ffa_w2), ln_g[i, 0], ln_b[i, 0], alpha)

        w_in_b = bf(w_in)
        b_in_i = b_in[i].reshape(1, -1)
        z = _matmul(xb, w_in_b, n=n_main, tm_target=1040, tn_target=512, name="in_proj", bias=b_in_i)
        logf = _gla_logf(
            xb,
            jnp.pad(w_in_b[:, n_main:n_gate], ((0, 0), (0, rank_pad))),
            jnp.pad(b_in_i[:, n_main:n_gate], ((0, 0), (0, rank_pad))),
            jnp.pad(gla_wa2[i].astype(BF16), ((0, rank_pad), (0, 0))),
            gla_ba[i].reshape(1, dk))

        pw = bf(pool_w)
        ps = pool_scale[i].reshape(1, pool_wd)
        ya = _pool_prompt(z, pw, ps, batch=batch, seq=seq, m=m, width=pool_wd)
        ya, nbuf = _pool_sample(z, state_pool[i].reshape(nsamp, POOL_BUF * pool_wd), pw, ps, ya,
                                row0=mp, nsamp=nsamp, width=pool_wd)
        pool_p.append(z[:mp, :pool_wd].reshape(batch, seq, pool_wd)[:, seq - POOL_BUF:])
        pool_s.append(nbuf.reshape(nsamp, POOL_BUF, pool_wd))

        lng = sgu_ln_g[i].reshape(1, sgu_wd)
        lnb = sgu_ln_b[i].reshape(1, sgu_wd)
        gd = sgu_wd // SGU_GROUPS
        yb = _sgu_prompt(z, lng, lnb, sgu_ws[i], sgu_bias[i].T, batch=batch, seq=seq, m=m, width=sgu_wd,
                         u_blk=u_blk, v_blk=v_blk)
        yb, vn = _sgu_sample(z, lng, lnb,
                             jnp.repeat(sgu_ws[i, :, 0, 0], gd).reshape(1, sgu_wd),
                             jnp.repeat(sgu_bias[i, :, 0], gd).reshape(1, sgu_wd),
                             yb, row0=mp, nsamp=nsamp, width=sgu_wd, u_blk=u_blk, v_blk=v_blk)
        sgu_s.append(vn.reshape(nsamp, 1, sgu_wd))

        ng = gla_norm_g[i].reshape(1, dv)
        blks = dict(hk=hk, hv=hv, q_blk=q_blk, k_blk=k_blk, v_blk=v2_blk, r_blk=r_blk)
        yc, s_p = _gla_prompt(z, logf, ng, batch=batch, seq=seq, m=m, **blks)
        yc, gla_s = _gla_sample(z, logf, ng, state_gla, i, yc, gla_s, row0=mp, nsamp=nsamp, **blks)
        gla_p.append(s_p)

        h = _up_gate(xb, ya, yb, yc, w_in_b[:, n_gate:], b_in_i[:, n_gate:], bf(w_up_a), bf(w_up_b), bf(w_up_c))
        y = _matmul(h, bf(w_o), n=d, tm_target=1040, tn_target=512, name="out_proj")
        x, xb = _res_ln(x, y, ln_g[i, 1], ln_b[i, 1], alpha=alpha, yscale=None)

        x, xb = _ffn(x, xb, bf(ffb_w1), bf(ffb_w3), bf(ffb_w2), ln_g[i, 2], ln_b[i, 2], alpha)

        pb = jnp.concatenate([p_prompt[i].reshape(mp, -1), p_sample[i].reshape(nsamp, -1)], axis=0).astype(BF16)
        y = _pe_embed(xb, pb, bf(pe_gate_w), bf(pe_w))
        x, xb = _res_ln(x, y, ln_g[i, 3], ln_b[i, 3], alpha=alpha, yscale=None)

    return (x[:mp].reshape(batch, seq, d), x[mp:].reshape(nsamp, 1, d),
            jnp.stack(pool_p), jnp.stack(gla_p).astype(state_gla.dtype),
            jnp.stack(pool_s), gla_s.astype(state_gla.dtype), jnp.stack(sgu_s))
```

```python
import functools
import math

import jax
import jax.numpy as jnp
from jax import lax
from jax.experimental import pallas as pl
from jax.experimental.pallas import tpu as pltpu

F32 = jnp.float32
BF16 = jnp.bfloat16

POOL_WINDOWS = (2, 4, 8, 16)
POOL_BUF = max(POOL_WINDOWS) - 1
POOL_HALO = 16
SGU_GROUPS = 8
SGU_CHUNK = 128
GLA_HEADS = 4
GLA_RANK = 16
GLA_TAU = 16.0
GLA_CHUNK = 64
EPS = 1e-5
PAST_LEN = 16384
SAMPLE_BLOCK = 8

V7X_VMEM_LIMIT_BYTES = 60 * 1024 * 1024


def _tile(dim, target, align):
    best = None
    t = align
    while t <= min(dim, target):
        if dim % t == 0:
            best = t
        t += align
    return dim if best is None else best


def _params(*sem):
    return pltpu.CompilerParams(dimension_semantics=sem, vmem_limit_bytes=V7X_VMEM_LIMIT_BYTES)


def _silu(x):
    return x * jax.nn.sigmoid(x)


def _dot(a, b):
    return jnp.dot(a, b, preferred_element_type=F32)


def _dot_nt(a, b):
    return lax.dot_general(a, b, (((1,), (1,)), ((), ())), preferred_element_type=F32)


def _dot_tn(a, b):
    return lax.dot_general(a, b, (((0,), (0,)), ((), ())), preferred_element_type=F32)


def _round_bf16(x):
    return x.astype(BF16).astype(F32)


def _w_spec(w, layer, k, tn, col_blk0=0):
    if w.ndim == 3:
        return pl.BlockSpec((None, k, tn), lambda i, j: (layer, 0, col_blk0 + j))
    return pl.BlockSpec((k, tn), lambda i, j: (0, col_blk0 + j))


def _resident_rows(tm, k):
    return pl.BlockSpec((tm, k), lambda i, j: (i, 0), pipeline_mode=pl.Buffered(1))


def _glu_kernel(x_ref, w1_ref, w3_ref, o_ref):
    x = x_ref[...]
    a = _dot(x, w1_ref[...].astype(BF16))
    b = _dot(x, w3_ref[...].astype(BF16))
    o_ref[...] = (_silu(a) * b).astype(o_ref.dtype)


def _ffn_up(xb, w1, w3, layer):
    m, k = xb.shape
    n = w1.shape[-1]
    tm = _tile(m, 2080, 16)
    tn = _tile(n, 256, 128)
    return pl.pallas_call(
        _glu_kernel,
        grid=(m // tm, n // tn),
        in_specs=[_resident_rows(tm, k),
                  _w_spec(w1, layer, k, tn),
                  _w_spec(w3, layer, k, tn)],
        out_specs=pl.BlockSpec((tm, tn), lambda i, j: (i, j)),
        out_shape=jax.ShapeDtypeStruct((m, n), BF16),
        compiler_params=_params("parallel", "arbitrary"),
        name="ffn_up",
    )(xb, w1, w3)


def _mm_kernel(x_ref, w_ref, o_ref):
    o_ref[...] = _dot(x_ref[...], w_ref[...].astype(BF16)).astype(o_ref.dtype)


def _mm_bias_kernel(x_ref, w_ref, b_ref, o_ref):
    o_ref[...] = (_dot(x_ref[...], w_ref[...].astype(BF16)) + b_ref[...]).astype(o_ref.dtype)


def _gate_pack_kernel(main_ref, halo_ref, o_ref, *, shift):
    cat = jnp.concatenate([main_ref[...], halo_ref[...]], axis=1)
    o_ref[...] = pltpu.roll(cat, cat.shape[1] - shift, 1)[:, :o_ref.shape[1]].astype(o_ref.dtype)


def _gate_pack(w_in, layer, col0, n):
    _, k, _ = w_in.shape
    lane = 128
    shift = col0 % lane
    base = col0 - shift
    tr = _tile(k, 1024, 16)
    tn = _tile(math.gcd(n, base), 1024, lane)
    assert shift > 0
    return pl.pallas_call(
        functools.partial(_gate_pack_kernel, shift=shift),
        grid=(k // tr, n // tn),
        in_specs=[pl.BlockSpec((None, tr, tn), lambda r, j: (layer, r, base // tn + j)),
                  pl.BlockSpec((None, tr, lane), lambda r, j: (layer, r, (base + (j + 1) * tn) // lane))],
        out_specs=pl.BlockSpec((tr, tn), lambda r, j: (r, j)),
        out_shape=jax.ShapeDtypeStruct((k, n), BF16),
        compiler_params=_params("parallel", "parallel"),
        name="gate_pack",
    )(w_in, w_in)


def _matmul(xb, w, *, n, tm_target, tn_target, name, layer=None, bias=None):
    m, k = xb.shape
    tm = _tile(m, tm_target, 16)
    tn = _tile(n, tn_target, 128)
    in_specs = [_resident_rows(tm, k), _w_spec(w, layer, k, tn)]
    args = [xb, w]
    kern = _mm_kernel
    if bias is not None:
        in_specs.append(pl.BlockSpec((1, tn), lambda i, j: (0, j)))
        args.append(bias)
        kern = _mm_bias_kernel
    return pl.pallas_call(
        kern,
        grid=(m // tm, n // tn),
        in_specs=in_specs,
        out_specs=pl.BlockSpec((tm, tn), lambda i, j: (i, j)),
        out_shape=jax.ShapeDtypeStruct((m, n), F32),
        compiler_params=_params("parallel", "arbitrary"),
        name=name,
    )(*args)


def _ln_kernel(x_ref, y_ref, g_ref, b_ref, of_ref, *maybe_ob_ref, alpha, yscale):
    y = y_ref[...]
    if yscale is not None:
        y = yscale * y
    t = alpha * x_ref[...] + y
    mu = jnp.mean(t, -1, keepdims=True)
    d = t - mu
    var = jnp.mean(d * d, -1, keepdims=True)
    o = d * lax.rsqrt(var + EPS) * g_ref[...] + b_ref[...]
    of_ref[...] = o
    for ob_ref in maybe_ob_ref:
        ob_ref[...] = o.astype(BF16)


def _res_ln(x, y, g, b, *, alpha, yscale, row0=0, rows=None, with_bf16=True):
    m, d = x.shape
    rows = m if rows is None else rows
    tm = _tile(rows, 208, 16)
    assert row0 % tm == 0
    rb = row0 // tm
    src = pl.BlockSpec((tm, d), lambda i: (rb + i, 0))
    dst = pl.BlockSpec((tm, d), lambda i: (i, 0))
    vec = pl.BlockSpec((1, d), lambda i: (0, 0))
    out_shape = [jax.ShapeDtypeStruct((rows, d), F32)]
    if with_bf16:
        out_shape.append(jax.ShapeDtypeStruct((rows, d), BF16))
    return pl.pallas_call(
        functools.partial(_ln_kernel, alpha=alpha, yscale=yscale),
        grid=(rows // tm,),
        in_specs=[src, src, vec, vec],
        out_specs=[dst] * len(out_shape),
        out_shape=out_shape,
        compiler_params=_params("parallel"),
        name="res_ln",
    )(x, y, g.reshape(1, d), b.reshape(1, d))


def _log_sigmoid(x):
    return jnp.minimum(x, 0.0) - jnp.log1p(jnp.exp(-jnp.abs(x)))


def _logf_kernel(x_ref, wl_ref, bl_ref, wa_ref, ba_ref, o_ref):
    a_lr = _dot(x_ref[...], wl_ref[...]) + bl_ref[...]
    t = _dot(a_lr.astype(BF16), wa_ref[...]) + ba_ref[...]
    o_ref[...] = _log_sigmoid(t) / GLA_TAU


def _gla_logf(xb, w_lr, b_lr, wa2, ba):
    m, k = xb.shape
    rp = w_lr.shape[1]
    dk = wa2.shape[1]
    tm = _tile(m, 1040, 16)
    return pl.pallas_call(
        _logf_kernel,
        grid=(m // tm,),
        in_specs=[pl.BlockSpec((tm, k), lambda i: (i, 0)),
                  pl.BlockSpec((k, rp), lambda i: (0, 0)),
                  pl.BlockSpec((1, rp), lambda i: (0, 0)),
                  pl.BlockSpec((rp, dk), lambda i: (0, 0)),
                  pl.BlockSpec((1, dk), lambda i: (0, 0))],
        out_specs=pl.BlockSpec((tm, dk), lambda i: (i, 0)),
        out_shape=jax.ShapeDtypeStruct((m, dk), F32),
        compiler_params=_params("parallel"),
        name="gla_logf",
    )(xb, w_lr, b_lr, wa2, ba)


def _upgate_kernel(x_ref, ya_ref, yb_ref, yc_ref, wga_ref, wgb_ref, wgc_ref, bga_ref, bgb_ref, bgc_ref,
                   wa_ref, wb_ref, wc_ref, o_ref):
    x = x_ref[...]
    acc = jax.nn.sigmoid(_dot(x, wga_ref[...]) + bga_ref[...]) * _dot(ya_ref[...], wa_ref[...].astype(BF16))
    acc = acc + jax.nn.sigmoid(_dot(x, wgb_ref[...]) + bgb_ref[...]) * _dot(yb_ref[...], wb_ref[...].astype(BF16))
    acc = acc + jax.nn.sigmoid(_dot(x, wgc_ref[...]) + bgc_ref[...]) * _dot(yc_ref[...], wc_ref[...].astype(BF16))
    o_ref[...] = acc.astype(o_ref.dtype)


def _up_gate(xb, ya, yb, yc, w_g, b_g, w_up_a, w_up_b, w_up_c, layer):
    m, k = xb.shape
    d = w_up_a.shape[-1]
    tm = _tile(m, 1040, 16)
    tn = _tile(d, 256, 128)
    nb = d // tn

    def rows(width):
        return _resident_rows(tm, width)

    def gate_cols(branch):
        return pl.BlockSpec((k, tn), lambda i, j: (0, branch * nb + j))

    def gate_bias(branch):
        return pl.BlockSpec((1, tn), lambda i, j: (0, branch * nb + j))

    def up_cols(width):
        return pl.BlockSpec((None, width, tn), lambda i, j: (layer, 0, j))

    return pl.pallas_call(
        _upgate_kernel,
        grid=(m // tm, nb),
        in_specs=[rows(k), rows(ya.shape[1]), rows(yb.shape[1]), rows(yc.shape[1]),
                  gate_cols(0), gate_cols(1), gate_cols(2),
                  gate_bias(0), gate_bias(1), gate_bias(2),
                  up_cols(ya.shape[1]), up_cols(yb.shape[1]), up_cols(yc.shape[1])],
        out_specs=pl.BlockSpec((tm, tn), lambda i, j: (i, j)),
        out_shape=jax.ShapeDtypeStruct((m, d), BF16),
        compiler_params=_params("parallel", "arbitrary"),
        name="up_gate",
    )(xb, ya, yb, yc, w_g, w_g, w_g, b_g, b_g, b_g, w_up_a, w_up_b, w_up_c)


def _pe_kernel(x_ref, p_ref, wg_ref, wp_ref, o_ref):
    o_ref[...] = (jax.nn.sigmoid(_dot(x_ref[...], wg_ref[...].astype(BF16)))
                  * _dot(p_ref[...], wp_ref[...].astype(BF16)))


def _pe_embed(xb, pb, w_gate, w_p, layer):
    m, k = xb.shape
    d = w_gate.shape[-1]
    kp = pb.shape[1]
    tm = _tile(m, 1664, 16)
    tn = _tile(d, 512, 128)
    return pl.pallas_call(
        _pe_kernel,
        grid=(m // tm, d // tn),
        in_specs=[_resident_rows(tm, k),
                  _resident_rows(tm, kp),
                  _w_spec(w_gate, layer, k, tn),
                  _w_spec(w_p, layer, kp, tn)],
        out_specs=pl.BlockSpec((tm, tn), lambda i, j: (i, j)),
        out_shape=jax.ShapeDtypeStruct((m, d), F32),
        compiler_params=_params("parallel", "arbitrary"),
        name="pe_embed",
    )(xb, pb, w_gate, w_p)


def _pool_prompt_kernel(a_ref, pw_ref, ps_ref, o_ref, ext_ref, *, tt, gd):
    t = pl.program_id(1)

    @pl.when(t == 0)
    def _():
        ext_ref[0:POOL_HALO, :] = jnp.zeros((POOL_HALO, ext_ref.shape[1]), F32)

    a = a_ref[...]
    ext_ref[POOL_HALO:POOL_HALO + tt, :] = a
    pos = t * tt + lax.broadcasted_iota(jnp.int32, (tt, 1), 0)
    for gi, w in enumerate(POOL_WINDOWS):
        sl = slice(gi * gd, (gi + 1) * gd)
        acc = a[:, sl]
        for s in range(1, w):
            acc = acc + ext_ref[POOL_HALO - s:POOL_HALO - s + tt, sl]
        cnt = jnp.minimum(pos + 1, w).astype(F32)
        d = acc / cnt - a[:, sl]
        y = _dot(d.astype(BF16), pw_ref[gi])
        o_ref[:, sl] = (y * ps_ref[:, sl]).astype(o_ref.dtype)
    ext_ref[0:POOL_HALO, :] = ext_ref[tt:tt + POOL_HALO, :]


def _pool_prompt(z, pool_w, pool_scale, *, batch, seq, m, width):
    tt = _tile(seq, 256, 16)
    nt = seq // tt
    gd = width // len(POOL_WINDOWS)
    return pl.pallas_call(
        functools.partial(_pool_prompt_kernel, tt=tt, gd=gd),
        grid=(batch, nt),
        in_specs=[pl.BlockSpec((tt, width), lambda b, t: (b * nt + t, 0)),
                  pl.BlockSpec(pool_w.shape, lambda b, t: (0, 0, 0)),
                  pl.BlockSpec((1, width), lambda b, t: (0, 0))],
        out_specs=pl.BlockSpec((tt, width), lambda b, t: (b * nt + t, 0)),
        out_shape=jax.ShapeDtypeStruct((m, width), BF16),
        scratch_shapes=[pltpu.VMEM((tt + POOL_HALO, width), F32)],
        compiler_params=_params("parallel", "arbitrary"),
        name="pool_prompt",
    )(z, pool_w, pool_scale)


def _pool_sample_kernel(a_ref, buf_ref, pw_ref, ps_ref, ya_in_ref, o_ref, nbuf_ref, *, width, gd):
    del ya_in_ref
    a = a_ref[...]
    for gi, w in enumerate(POOL_WINDOWS):
        sl = slice(gi * gd, (gi + 1) * gd)
        acc = a[:, sl]
        for s in range(1, w):
            r = POOL_BUF - s
            acc = acc + buf_ref[:, r * width + gi * gd:r * width + (gi + 1) * gd]
        d = acc / float(min(PAST_LEN + 1, w)) - a[:, sl]
        y = _dot(d.astype(BF16), pw_ref[gi])
        o_ref[:, sl] = (y * ps_ref[:, sl]).astype(o_ref.dtype)
    nbuf_ref[:, 0:(POOL_BUF - 1) * width] = buf_ref[:, width:POOL_BUF * width]
    nbuf_ref[:, (POOL_BUF - 1) * width:POOL_BUF * width] = a


def _pool_sample(z, buf_flat, layer, pool_w, pool_scale, ya, *, row0, nsamp, width):
    ns = SAMPLE_BLOCK
    gd = width // len(POOL_WINDOWS)
    rb = row0 // ns
    ya, nbuf = pl.pallas_call(
        functools.partial(_pool_sample_kernel, width=width, gd=gd),
        grid=(nsamp // ns,),
        in_specs=[pl.BlockSpec((ns, width), lambda i: (rb + i, 0)),
                  pl.BlockSpec((None, ns, POOL_BUF * width), lambda i: (layer, i, 0)),
                  pl.BlockSpec(pool_w.shape, lambda i: (0, 0, 0)),
                  pl.BlockSpec((1, width), lambda i: (0, 0)),
                  pl.BlockSpec(memory_space=pl.ANY)],
        out_specs=[pl.BlockSpec((ns, width), lambda i: (rb + i, 0)),
                   pl.BlockSpec((ns, POOL_BUF * width), lambda i: (i, 0))],
        out_shape=[jax.ShapeDtypeStruct(ya.shape, ya.dtype),
                   jax.ShapeDtypeStruct(buf_flat.shape[1:], F32)],
        input_output_aliases={4: 0},
        compiler_params=_params("parallel"),
        name="pool_sample",
    )(z, buf_flat, pool_w, pool_scale, ya)
    return ya, nbuf


def _row_layer_norm(v, g, b):
    mu = jnp.mean(v, -1, keepdims=True)
    d = v - mu
    var = jnp.mean(d * d, -1, keepdims=True)
    return d * lax.rsqrt(var + EPS) * g + b


def _sgu_prompt_kernel(u_ref, v_ref, lng_ref, lnb_ref, ws_ref, biast_ref, o_ref, *, nchunk, gd):
    c = SGU_CHUNK
    tril = (lax.broadcasted_iota(jnp.int32, (c, c), 0) >= lax.broadcasted_iota(jnp.int32, (c, c), 1)).astype(F32)
    w_m = [(ws_ref[g] * tril).astype(BF16) for g in range(SGU_GROUPS)]
    for ci in range(nchunk):
        rows = slice(ci * c, (ci + 1) * c)
        vn = _row_layer_norm(v_ref[rows, :], lng_ref[...], lnb_ref[...]).astype(BF16)
        for g in range(SGU_GROUPS):
            sl = slice(g * gd, (g + 1) * gd)
            mixed = _dot(w_m[g], vn[:, sl]) + biast_ref[:, g:g + 1]
            o_ref[rows, sl] = (u_ref[rows, sl] * mixed).astype(o_ref.dtype)


def _sgu_prompt(z, ln_g, ln_b, ws, bias_t, *, batch, seq, m, width, u_blk, v_blk):
    assert seq % SGU_CHUNK == 0
    rows = _tile(seq, 512, SGU_CHUNK)
    nt = seq // rows
    gd = width // SGU_GROUPS
    return pl.pallas_call(
        functools.partial(_sgu_prompt_kernel, nchunk=rows // SGU_CHUNK, gd=gd),
        grid=(batch * nt,),
        in_specs=[pl.BlockSpec((rows, width), lambda i: (i, u_blk)),
                  pl.BlockSpec((rows, width), lambda i: (i, v_blk)),
                  pl.BlockSpec((1, width), lambda i: (0, 0)),
                  pl.BlockSpec((1, width), lambda i: (0, 0)),
                  pl.BlockSpec(ws.shape, lambda i: (0, 0, 0)),
                  pl.BlockSpec(bias_t.shape, lambda i: (0, 0))],
        out_specs=pl.BlockSpec((rows, width), lambda i: (i, 0)),
        out_shape=jax.ShapeDtypeStruct((m, width), BF16),
        compiler_params=_params("parallel"),
        name="sgu_prompt",
    )(z, z, ln_g, ln_b, ws, bias_t)


def _sgu_sample_kernel(u_ref, v_ref, lng_ref, lnb_ref, w0_ref, b0_ref, yb_in_ref, o_ref, vn_ref):
    del yb_in_ref
    vn = _row_layer_norm(v_ref[...], lng_ref[...], lnb_ref[...])
    vn_ref[...] = vn
    mixed = _round_bf16(w0_ref[...]) * _round_bf16(vn) + b0_ref[...]
    o_ref[...] = (u_ref[...] * mixed).astype(o_ref.dtype)


def _sgu_sample(z, ln_g, ln_b, w0, b0, yb, *, row0, nsamp, width, u_blk, v_blk):
    ns = SAMPLE_BLOCK
    rb = row0 // ns
    vec = pl.BlockSpec((1, width), lambda i: (0, 0))
    yb, vn = pl.pallas_call(
        _sgu_sample_kernel,
        grid=(nsamp // ns,),
        in_specs=[pl.BlockSpec((ns, width), lambda i: (rb + i, u_blk)),
                  pl.BlockSpec((ns, width), lambda i: (rb + i, v_blk)),
                  vec, vec, vec, vec,
                  pl.BlockSpec(memory_space=pl.ANY)],
        out_specs=[pl.BlockSpec((ns, width), lambda i: (rb + i, 0)),
                   pl.BlockSpec((ns, width), lambda i: (i, 0))],
        out_shape=[jax.ShapeDtypeStruct(yb.shape, yb.dtype),
                   jax.ShapeDtypeStruct((nsamp, width), F32)],
        input_output_aliases={6: 0},
        compiler_params=_params("parallel"),
        name="sgu_sample",
    )(z, z, ln_g, ln_b, w0, b0, yb)
    return yb, vn


def _cumsum_rows(x):
    rows = x.shape[0]
    row = lax.broadcasted_iota(jnp.int32, x.shape, 0)
    s = 1
    while s < rows:
        x = x + jnp.where(row >= s, pltpu.roll(x, s, 0), 0.0)
        s *= 2
    return x


def _gla_out(o, norm_g, r):
    o = o * lax.rsqrt(jnp.mean(o * o, -1, keepdims=True) + EPS) * norm_g
    return o * _silu(r)


def _gla_prompt_kernel(q_ref, k_ref, v01_ref, v23_ref, g_ref, r01_ref, r23_ref, ng_ref, o_ref, s_ref, st_ref, *,
                       nchunk, scale, hk, hv):
    t = pl.program_id(1)
    c = GLA_CHUNK

    @pl.when(t == 0)
    def _():
        st_ref[...] = jnp.zeros(st_ref.shape, F32)

    causal = (lax.broadcasted_iota(jnp.int32, (c, c), 0) >= lax.broadcasted_iota(jnp.int32, (c, c), 1)).astype(F32)
    v_refs = (v01_ref, v23_ref)
    r_refs = (r01_ref, r23_ref)
    for ci in range(nchunk):
        rows = slice(ci * c, (ci + 1) * c)
        for h in range(GLA_HEADS):
            kcols = slice(h * hk, (h + 1) * hk)
            vcols = slice((h % 2) * hv, (h % 2 + 1) * hv)
            ocols = slice(h * hv, (h + 1) * hv)
            b = _cumsum_rows(g_ref[rows, kcols])
            b_end = b[c - 1:c, :]
            k = k_ref[rows, kcols]
            q_in = ((q_ref[rows, kcols] * scale) * jnp.exp(b)).astype(BF16)
            k_in = (k * jnp.exp(-b)).astype(BF16)
            k_dec = (k * jnp.exp(b_end - b)).astype(BF16)
            vb = v_refs[h // 2][rows, vcols].astype(BF16)
            att = _dot_nt(q_in, k_in) * causal
            st = st_ref[h]
            o = _dot(att.astype(BF16), vb) + _dot_nt(q_in, st.astype(BF16))
            st_ref[h] = st * jnp.exp(b_end) + _dot_tn(vb, k_dec)
            o_ref[rows, ocols] = _gla_out(o, ng_ref[:, ocols], r_refs[h // 2][rows, vcols]).astype(o_ref.dtype)

    @pl.when(t == pl.num_programs(1) - 1)
    def _():
        for h in range(GLA_HEADS):
            s_ref[0, h] = st_ref[h].T


def _gla_prompt(z, logf, norm_g, *, batch, seq, m, hk, hv, q_off):
    assert seq % GLA_CHUNK == 0 and GLA_HEADS == 4
    rows = _tile(seq, 256, GLA_CHUNK)
    nt = seq // rows
    heads = GLA_HEADS
    dk, dv2 = heads * hk, 2 * hv
    v_off = q_off + 2 * dk
    assert q_off % dk == 0 and v_off % dv2 == 0

    def cols(width, off):
        return pl.BlockSpec((rows, width), lambda n, t: (n * nt + t, off // width))

    yc, s = pl.pallas_call(
        functools.partial(_gla_prompt_kernel, nchunk=rows // GLA_CHUNK, scale=hk ** -0.5, hk=hk, hv=hv),
        grid=(batch, nt),
        in_specs=[cols(dk, q_off), cols(dk, q_off + dk),
                  cols(dv2, v_off), cols(dv2, v_off + dv2),
                  cols(dk, 0),
                  cols(dv2, v_off + 2 * dv2), cols(dv2, v_off + 3 * dv2),
                  pl.BlockSpec((1, heads * hv), lambda n, t: (0, 0))],
        out_specs=[pl.BlockSpec((rows, heads * hv), lambda n, t: (n * nt + t, 0)),
                   pl.BlockSpec((1, heads, hk, hv), lambda n, t: (n, 0, 0, 0))],
        out_shape=[jax.ShapeDtypeStruct((m, heads * hv), BF16),
                   jax.ShapeDtypeStruct((batch, heads, hk, hv), F32)],
        scratch_shapes=[pltpu.VMEM((heads, hv, hk), F32)],
        compiler_params=_params("parallel", "arbitrary"),
        name="gla_prompt",
    )(z, z, z, z, logf, z, z, norm_g)
    return yc, s


def _gla_sample_kernel(*refs, scale):
    q_ref, k_ref, v_ref, g_ref, r_ref, ng_ref, s_ref, yc_in_ref = refs[:8]
    o_ref, so_ref = refs[-2:]
    del yc_in_ref
    ns = SAMPLE_BLOCK
    g = g_ref[...]
    e = jnp.exp(g)
    k = k_ref[...]
    q_in = _round_bf16((q_ref[...] * scale) * e)
    k_in = _round_bf16(k * jnp.exp(-g))
    k_dec = _round_bf16(k * jnp.exp(g - g))
    vb = _round_bf16(v_ref[...])
    att = jnp.sum(q_in * k_in, -1, keepdims=True)
    hk = g.shape[1]
    cols = jnp.concatenate([e, k_dec, q_in, jnp.zeros((128 - 3 * ns, hk), F32)], axis=0).T
    outs = []
    for j in range(ns):
        s = s_ref[0, j, 0]
        so_ref[0, j, 0] = s * cols[:, j:j + 1] + cols[:, ns + j:ns + j + 1] * vb[j:j + 1, :]
        outs.append(jnp.sum(cols[:, 2 * ns + j:2 * ns + j + 1] * _round_bf16(s), axis=0, keepdims=True))
    o = jnp.concatenate(outs, axis=0) + _round_bf16(att) * vb
    o_ref[...] = _gla_out(o, ng_ref[...], r_ref[...]).astype(o_ref.dtype)


def _gla_sample(z, logf, norm_g, state, layer, yc, prev_out, *, row0, nsamp, hk, hv, q_blk, k_blk, v_blk, r_blk):
    ns = SAMPLE_BLOCK
    rb = row0 // ns
    heads = GLA_HEADS
    st_spec = pl.BlockSpec((1, ns, 1, hk, hv), lambda i, h: (layer, i, h, 0, 0))
    in_specs = [pl.BlockSpec((ns, hk), lambda i, h: (rb + i, q_blk + h)),
                pl.BlockSpec((ns, hk), lambda i, h: (rb + i, k_blk + h)),
                pl.BlockSpec((ns, hv), lambda i, h: (rb + i, v_blk + h)),
                pl.BlockSpec((ns, hk), lambda i, h: (rb + i, h)),
                pl.BlockSpec((ns, hv), lambda i, h: (rb + i, r_blk + h)),
                pl.BlockSpec((1, hv), lambda i, h: (0, h)),
                st_spec,
                pl.BlockSpec(memory_space=pl.ANY)]
    args = [z, z, z, logf, z, norm_g, state, yc]
    aliases = {7: 0}
    if prev_out is not None:
        in_specs.append(pl.BlockSpec(memory_space=pl.ANY))
        args.append(prev_out)
        aliases[8] = 1
    yc, s_out = pl.pallas_call(
        functools.partial(_gla_sample_kernel, scale=hk ** -0.5),
        grid=(nsamp // ns, heads),
        in_specs=in_specs,
        out_specs=[pl.BlockSpec((ns, hv), lambda i, h: (rb + i, h)), st_spec],
        out_shape=[jax.ShapeDtypeStruct(yc.shape, yc.dtype),
                   jax.ShapeDtypeStruct(state.shape, F32)],
        input_output_aliases=aliases,
        compiler_params=_params("parallel", "arbitrary"),
        name="gla_sample",
    )(*args)
    return yc, s_out


def _ffn(x, xb, w1, w3, w2, layer, g, b, alpha):
    h = _ffn_up(xb, w1, w3, layer)
    y = _matmul(h, w2, n=w2.shape[2], tm_target=1040, tn_target=256, name="ffn_down", layer=layer)
    return _res_ln(x, y, g, b, alpha=alpha, yscale=0.5)


def kernel(x_prompt, x_sample, state_pool, state_gla, p_prompt, p_sample, ffa_w1, ffa_w3, ffa_w2, ffb_w1, ffb_w3, ffb_w2, ln_g, ln_b, w_in, b_in, pool_w, pool_scale, sgu_ln_g, sgu_ln_b, sgu_ws, sgu_bias, gla_wa2, gla_ba, gla_norm_g, w_up_a, w_up_b, w_up_c, w_o, pe_w, pe_gate_w):
    depth = ffa_w1.shape[0]
    batch, seq, d = x_prompt.shape
    nsamp, dec_seq, _ = x_sample.shape
    assert dec_seq == 1 and nsamp % SAMPLE_BLOCK == 0
    mp = batch * seq
    m = mp + nsamp
    assert mp % SAMPLE_BLOCK == 0
    alpha = (2.0 * depth) ** 0.25

    pool_wd = pool_w.shape[1] * pool_w.shape[2]
    sgu_wd = sgu_ln_g.shape[1]
    dk = gla_wa2.shape[2]
    dv = gla_norm_g.shape[1]
    hk, hv = dk // GLA_HEADS, dv // GLA_HEADS
    n_main = pool_wd + 2 * sgu_wd + 2 * dk + 2 * dv
    n_gate = n_main + GLA_RANK
    assert w_in.shape[2] == n_gate + 3 * d
    assert pool_wd == sgu_wd
    u_blk, v_blk = pool_wd // sgu_wd, pool_wd // sgu_wd + 1
    q_off = pool_wd + 2 * sgu_wd
    q_blk, k_blk = q_off // hk, (q_off + dk) // hk
    v2_blk, r_blk = (q_off + 2 * dk) // hv, (q_off + 2 * dk + dv) // hv

    x = jnp.concatenate([x_prompt.reshape(mp, d), x_sample.reshape(nsamp, d)], axis=0)
    xb = x.astype(BF16)

    pool_p, gla_p, pool_s, sgu_s = [], [], [], []
    gla_s = None
    rank_pad = 128 - GLA_RANK
    state_pool_flat = state_pool.reshape(depth, nsamp, POOL_BUF * pool_wd)
    for i in range(depth):
        x, xb = _ffn(x, xb, ffa_w1, ffa_w3, ffa_w2, i, ln_g[i, 0], ln_b[i, 0], alpha)

        b_in_i = b_in[i].reshape(1, -1)
        z = _matmul(xb, w_in, n=n_main, tm_target=1664, tn_target=512, name="in_proj", layer=i, bias=b_in_i)
        logf = _gla_logf(
            xb,
            jnp.pad(w_in[i, :, n_main:n_gate].astype(BF16), ((0, 0), (0, rank_pad))),
            jnp.pad(b_in_i[:, n_main:n_gate], ((0, 0), (0, rank_pad))),
            jnp.pad(gla_wa2[i].astype(BF16), ((0, rank_pad), (0, 0))),
            gla_ba[i].reshape(1, dk))

        pw = pool_w[i].astype(BF16)
        ps = pool_scale[i].reshape(1, pool_wd)
        ya = _pool_prompt(z, pw, ps, batch=batch, seq=seq, m=m, width=pool_wd)
        ya, nbuf = _pool_sample(z, state_pool_flat, i, pw, ps, ya, row0=mp, nsamp=nsamp, width=pool_wd)
        pool_p.append(z[:mp, :pool_wd].reshape(batch, seq, pool_wd)[:, seq - POOL_BUF:])
        pool_s.append(nbuf.reshape(nsamp, POOL_BUF, pool_wd))

        lng = sgu_ln_g[i].reshape(1, sgu_wd)
        lnb = sgu_ln_b[i].reshape(1, sgu_wd)
        gd = sgu_wd // SGU_GROUPS
        yb = _sgu_prompt(z, lng, lnb, sgu_ws[i], sgu_bias[i].T, batch=batch, seq=seq, m=m, width=sgu_wd,
                         u_blk=u_blk, v_blk=v_blk)
        yb, vn = _sgu_sample(z, lng, lnb,
                             jnp.repeat(sgu_ws[i, :, 0, 0], gd).reshape(1, sgu_wd),
                             jnp.repeat(sgu_bias[i, :, 0], gd).reshape(1, sgu_wd),
                             yb, row0=mp, nsamp=nsamp, width=sgu_wd, u_blk=u_blk, v_blk=v_blk)
        sgu_s.append(vn.reshape(nsamp, 1, sgu_wd))

        ng = gla_norm_g[i].reshape(1, dv)
        blks = dict(hk=hk, hv=hv, q_blk=q_blk, k_blk=k_blk, v_blk=v2_blk, r_blk=r_blk)
        yc, s_p = _gla_prompt(z, logf, ng, batch=batch, seq=seq, m=m, hk=hk, hv=hv, q_off=q_off)
        yc, gla_s = _gla_sample(z, logf, ng, state_gla, i, yc, gla_s, row0=mp, nsamp=nsamp, **blks)
        gla_p.append(s_p)

        w_g = _gate_pack(w_in, i, n_gate, 3 * d)
        h = _up_gate(xb, ya, yb, yc, w_g, b_in_i[:, n_gate:], w_up_a, w_up_b, w_up_c, i)
        y = _matmul(h, w_o, n=d, tm_target=1664, tn_target=512, name="out_proj", layer=i)
        x, xb = _res_ln(x, y, ln_g[i, 1], ln_b[i, 1], alpha=alpha, yscale=None)

        x, xb = _ffn(x, xb, ffb_w1, ffb_w3, ffb_w2, i, ln_g[i, 2], ln_b[i, 2], alpha)

        pb = jnp.concatenate([p_prompt[i].reshape(mp, -1), p_sample[i].reshape(nsamp, -1)], axis=0).astype(BF16)
        y = _pe_embed(xb, pb, pe_gate_w, pe_w, i)
        ln = functools.partial(_res_ln, x, y, ln_g[i, 3], ln_b[i, 3], alpha=alpha, yscale=None)
        if i + 1 < depth:
            x, xb = ln()
        else:
            (x_p,), (x_s,) = ln(rows=mp, with_bf16=False), ln(row0=mp, rows=nsamp, with_bf16=False)

    return (x_p.reshape(batch, seq, d), x_s.reshape(nsamp, 1, d),
            jnp.stack(pool_p), jnp.stack(gla_p).astype(state_gla.dtype),
            jnp.stack(pool_s), gla_s.astype(state_gla.dtype), jnp.stack(sgu_s))
```

```python
import functools

import jax
import jax.numpy as jnp
from jax import lax
from jax.experimental import pallas as pl
from jax.experimental.pallas import tpu as pltpu

F32 = jnp.float32
BF16 = jnp.bfloat16

POOL_WINDOWS = (2, 4, 8, 16)
POOL_BUF = max(POOL_WINDOWS) - 1
POOL_HALO = 16
SGU_GROUPS = 8
SGU_CHUNK = 128
GLA_HEADS = 4
GLA_RANK = 16
GLA_TAU = 16.0
GLA_CHUNK = 64
EPS = 1e-5
PAST_LEN = 16384
SAMPLE_BLOCK = 8

V7X_VMEM_LIMIT_BYTES = 60 * 1024 * 1024


def _tile(dim, target, align):
    best = None
    t = align
    while t <= min(dim, target):
        if dim % t == 0:
            best = t
        t += align
    return dim if best is None else best


def _params(*sem):
    return pltpu.CompilerParams(dimension_semantics=sem, vmem_limit_bytes=V7X_VMEM_LIMIT_BYTES)


def _silu(x):
    return x * jax.nn.sigmoid(x)


def _dot(a, b):
    return jnp.dot(a, b, preferred_element_type=F32)


def _dot_nt(a, b):
    return lax.dot_general(a, b, (((1,), (1,)), ((), ())), preferred_element_type=F32)


def _dot_tn(a, b):
    return lax.dot_general(a, b, (((0,), (0,)), ((), ())), preferred_element_type=F32)


def _round_bf16(x):
    return x.astype(BF16).astype(F32)


def _w_spec(w, layer, k, tn):
    assert w.ndim == 3 and w.shape[1] == k
    return pl.BlockSpec((None, k, tn), lambda i, j: (layer, 0, j))


def _resident_rows(tm, k):
    return pl.BlockSpec((tm, k), lambda i, j: (i, 0), pipeline_mode=pl.Buffered(1))


def _glu_kernel(x_ref, w1_ref, w3_ref, o_ref):
    x = x_ref[...]
    a = _dot(x, w1_ref[...].astype(BF16))
    b = _dot(x, w3_ref[...].astype(BF16))
    o_ref[...] = (_silu(a) * b).astype(o_ref.dtype)


def _ffn_up(xb, w1, w3, layer):
    m, k = xb.shape
    n = w1.shape[-1]
    tm = _tile(m, 2080, 16)
    tn = _tile(n, 256, 128)
    return pl.pallas_call(
        _glu_kernel,
        grid=(m // tm, n // tn),
        in_specs=[_resident_rows(tm, k),
                  _w_spec(w1, layer, k, tn),
                  _w_spec(w3, layer, k, tn)],
        out_specs=pl.BlockSpec((tm, tn), lambda i, j: (i, j)),
        out_shape=jax.ShapeDtypeStruct((m, n), BF16),
        compiler_params=_params("parallel", "arbitrary"),
        name="ffn_up",
    )(xb, w1, w3)


def _mm_kernel(x_ref, w_ref, o_ref):
    o_ref[...] = _dot(x_ref[...], w_ref[...].astype(BF16)).astype(o_ref.dtype)


def _mm_nt_bias_kernel(x_ref, wt_ref, b_ref, o_ref):
    o_ref[...] = (_dot_nt(x_ref[...], wt_ref[...].astype(BF16)) + b_ref[...]).astype(o_ref.dtype)


def _cast_kernel(w_ref, o_ref):
    o_ref[...] = w_ref[...].astype(o_ref.dtype)


def _cast_rows(w, layer, row0, rows):
    _, _, k = w.shape
    tr = _tile(rows, 512, 16)
    assert row0 % 8 == 0
    return pl.pallas_call(
        _cast_kernel,
        grid=(rows // tr,),
        in_specs=[pl.BlockSpec((None, pl.Element(tr), pl.Element(k)),
                               lambda r: (layer, pl.multiple_of(row0 + r * tr, 8), 0))],
        out_specs=pl.BlockSpec((tr, k), lambda r: (r, 0)),
        out_shape=jax.ShapeDtypeStruct((rows, k), BF16),
        compiler_params=_params("parallel"),
        name="cast_rows",
    )(w)


def _matmul(xb, w, layer, *, tm_target, tn_target, name):
    m, k = xb.shape
    n = w.shape[2]
    tm = _tile(m, tm_target, 16)
    tn = _tile(n, tn_target, 128)
    return pl.pallas_call(
        _mm_kernel,
        grid=(m // tm, n // tn),
        in_specs=[_resident_rows(tm, k), _w_spec(w, layer, k, tn)],
        out_specs=pl.BlockSpec((tm, tn), lambda i, j: (i, j)),
        out_shape=jax.ShapeDtypeStruct((m, n), F32),
        compiler_params=_params("parallel", "arbitrary"),
        name=name,
    )(xb, w)


def _in_proj(xb, w_t, bias, layer, n):
    m, k = xb.shape
    tm = _tile(m, 1664, 16)
    tn = _tile(n, 512, 128)
    return pl.pallas_call(
        _mm_nt_bias_kernel,
        grid=(m // tm, n // tn),
        in_specs=[_resident_rows(tm, k),
                  pl.BlockSpec((None, tn, k), lambda i, j: (layer, j, 0)),
                  pl.BlockSpec((1, tn), lambda i, j: (0, j))],
        out_specs=pl.BlockSpec((tm, tn), lambda i, j: (i, j)),
        out_shape=jax.ShapeDtypeStruct((m, n), F32),
        compiler_params=_params("parallel", "arbitrary"),
        name="in_proj",
    )(xb, w_t, bias)


def _ln_alias_kernel(x_ref, y_ref, g_ref, b_ref, *refs, alpha, yscale, n_alias):
    _ln_kernel(x_ref, y_ref, g_ref, b_ref, *refs[n_alias:], alpha=alpha, yscale=yscale)


def _ln_kernel(x_ref, y_ref, g_ref, b_ref, of_ref, *maybe_ob_ref, alpha, yscale):
    y = y_ref[...]
    if yscale is not None:
        y = yscale * y
    t = alpha * x_ref[...] + y
    mu = jnp.mean(t, -1, keepdims=True)
    d = t - mu
    var = jnp.mean(d * d, -1, keepdims=True)
    o = d * lax.rsqrt(var + EPS) * g_ref[...] + b_ref[...]
    of_ref[...] = o
    for ob_ref in maybe_ob_ref:
        ob_ref[...] = o.astype(BF16)


def _res_ln(x, y, g, b, *, alpha, yscale, rows=None, x_row0=0, y_row0=0, out_row0=0, out_rows=None,
            with_bf16=True, into=None):
    d = x.shape[1]
    rows = x.shape[0] if rows is None else rows
    out_rows = rows if out_rows is None else out_rows
    tm = _tile(rows, 208, 16)
    assert x_row0 % tm == 0 and y_row0 % tm == 0 and out_row0 % tm == 0

    def at(row0):
        return pl.BlockSpec((tm, d), lambda i: (row0 // tm + i, 0))

    vec = pl.BlockSpec((1, d), lambda i: (0, 0))
    out_shape = [jax.ShapeDtypeStruct((out_rows, d), F32)]
    if with_bf16:
        out_shape.append(jax.ShapeDtypeStruct((out_rows, d), BF16))
    in_specs = [at(x_row0), at(y_row0), vec, vec]
    args = [x, y, g.reshape(1, d), b.reshape(1, d)]
    aliases = {}
    if into is not None:
        in_specs += [pl.BlockSpec(memory_space=pl.ANY)] * len(into)
        aliases = {len(args) + n: n for n in range(len(into))}
        args += list(into)
    return pl.pallas_call(
        functools.partial(_ln_alias_kernel, alpha=alpha, yscale=yscale, n_alias=len(aliases)),
        grid=(rows // tm,),
        in_specs=in_specs,
        out_specs=[at(out_row0)] * len(out_shape),
        out_shape=out_shape,
        input_output_aliases=aliases,
        compiler_params=_params("parallel"),
        name="res_ln",
    )(*args)


def _log_sigmoid(x):
    return jnp.minimum(x, 0.0) - jnp.log1p(jnp.exp(-jnp.abs(x)))


def _logf_kernel(x_ref, wt_ref, bl_ref, wa_ref, ba_ref, o_ref):
    a_lr = _dot_nt(x_ref[...], wt_ref[...].astype(BF16)) + bl_ref[...]
    t = _dot(a_lr.astype(BF16), wa_ref[...]) + ba_ref[...]
    o_ref[...] = _log_sigmoid(t) / GLA_TAU


def _gla_logf(xb, w_in_t, b_in, layer, col0, wa2_pad, ba):
    m, k = xb.shape
    lane, dk = wa2_pad.shape
    assert col0 % lane == 0
    tm = _tile(m, 1040, 16)
    return pl.pallas_call(
        _logf_kernel,
        grid=(m // tm,),
        in_specs=[pl.BlockSpec((tm, k), lambda i: (i, 0)),
                  pl.BlockSpec((None, lane, k), lambda i: (layer, col0 // lane, 0)),
                  pl.BlockSpec((1, lane), lambda i: (0, col0 // lane)),
                  pl.BlockSpec((lane, dk), lambda i: (0, 0)),
                  pl.BlockSpec((1, dk), lambda i: (0, 0))],
        out_specs=pl.BlockSpec((tm, dk), lambda i: (i, 0)),
        out_shape=jax.ShapeDtypeStruct((m, dk), F32),
        compiler_params=_params("parallel"),
        name="gla_logf",
    )(xb, w_in_t, b_in, wa2_pad, ba)


def _upgate_kernel(x_ref, ya_ref, yb_ref, yc_ref, wga_ref, wgb_ref, wgc_ref, bga_ref, bgb_ref, bgc_ref,
                   wa_ref, wb_ref, wc_ref, o_ref):
    x = x_ref[...]
    acc = jax.nn.sigmoid(_dot_nt(x, wga_ref[...]) + bga_ref[...]) * _dot(ya_ref[...], wa_ref[...].astype(BF16))
    acc = acc + jax.nn.sigmoid(_dot_nt(x, wgb_ref[...]) + bgb_ref[...]) * _dot(yb_ref[...], wb_ref[...].astype(BF16))
    acc = acc + jax.nn.sigmoid(_dot_nt(x, wgc_ref[...]) + bgc_ref[...]) * _dot(yc_ref[...], wc_ref[...].astype(BF16))
    o_ref[...] = acc.astype(o_ref.dtype)


def _up_gate(xb, ya, yb, yc, w_g, b_g, w_up_a, w_up_b, w_up_c, layer):
    m, k = xb.shape
    d = w_up_a.shape[-1]
    tm = _tile(m, 1040, 16)
    tn = _tile(d, 256, 128)
    nb = d // tn

    def rows(width):
        return _resident_rows(tm, width)

    def gate_cols(branch):
        return pl.BlockSpec((tn, k), lambda i, j: (branch * nb + j, 0))

    def gate_bias(branch):
        return pl.BlockSpec((1, tn), lambda i, j: (0, branch * nb + j))

    def up_cols(width):
        return pl.BlockSpec((None, width, tn), lambda i, j: (layer, 0, j))

    return pl.pallas_call(
        _upgate_kernel,
        grid=(m // tm, nb),
        in_specs=[rows(k), rows(ya.shape[1]), rows(yb.shape[1]), rows(yc.shape[1]),
                  gate_cols(0), gate_cols(1), gate_cols(2),
                  gate_bias(0), gate_bias(1), gate_bias(2),
                  up_cols(ya.shape[1]), up_cols(yb.shape[1]), up_cols(yc.shape[1])],
        out_specs=pl.BlockSpec((tm, tn), lambda i, j: (i, j)),
        out_shape=jax.ShapeDtypeStruct((m, d), BF16),
        compiler_params=_params("parallel", "arbitrary"),
        name="up_gate",
    )(xb, ya, yb, yc, w_g, w_g, w_g, b_g, b_g, b_g, w_up_a, w_up_b, w_up_c)


def _pe_kernel(x_ref, p_ref, wg_ref, wp_ref, o_ref):
    o_ref[...] = (jax.nn.sigmoid(_dot(x_ref[...], wg_ref[...].astype(BF16)))
                  * _dot(p_ref[...], wp_ref[...].astype(BF16)))


def _pe_embed(xb, pb, w_gate, w_p, layer):
    m, k = xb.shape
    d = w_gate.shape[-1]
    kp = pb.shape[1]
    tm = _tile(m, 1664, 16)
    tn = _tile(d, 512, 128)
    return pl.pallas_call(
        _pe_kernel,
        grid=(m // tm, d // tn),
        in_specs=[_resident_rows(tm, k),
                  _resident_rows(tm, kp),
                  _w_spec(w_gate, layer, k, tn),
                  _w_spec(w_p, layer, kp, tn)],
        out_specs=pl.BlockSpec((tm, tn), lambda i, j: (i, j)),
        out_shape=jax.ShapeDtypeStruct((m, d), F32),
        compiler_params=_params("parallel", "arbitrary"),
        name="pe_embed",
    )(xb, pb, w_gate, w_p)


def _pool_prompt_kernel(a_ref, pw_ref, ps_ref, o_ref, tail_ref, ext_ref, *, tt, gd):
    t = pl.program_id(1)

    @pl.when(t == pl.num_programs(1) - 1)
    def _():
        tail_ref[...] = a_ref[tt - POOL_HALO:tt, :]

    @pl.when(t == 0)
    def _():
        ext_ref[0:POOL_HALO, :] = jnp.zeros((POOL_HALO, ext_ref.shape[1]), F32)

    a = a_ref[...]
    ext_ref[POOL_HALO:POOL_HALO + tt, :] = a
    pos = t * tt + lax.broadcasted_iota(jnp.int32, (tt, 1), 0)
    for gi, w in enumerate(POOL_WINDOWS):
        sl = slice(gi * gd, (gi + 1) * gd)
        acc = a[:, sl]
        for s in range(1, w):
            acc = acc + ext_ref[POOL_HALO - s:POOL_HALO - s + tt, sl]
        cnt = jnp.minimum(pos + 1, w).astype(F32)
        d = acc / cnt - a[:, sl]
        y = _dot(d.astype(BF16), pw_ref[gi])
        o_ref[:, sl] = (y * ps_ref[:, sl]).astype(o_ref.dtype)
    ext_ref[0:POOL_HALO, :] = ext_ref[tt:tt + POOL_HALO, :]


def _pool_prompt(z, pool_w, pool_scale, *, batch, seq, m, width):
    tt = _tile(seq, 256, 16)
    nt = seq // tt
    gd = width // len(POOL_WINDOWS)
    return pl.pallas_call(
        functools.partial(_pool_prompt_kernel, tt=tt, gd=gd),
        grid=(batch, nt),
        in_specs=[pl.BlockSpec((tt, width), lambda b, t: (b * nt + t, 0)),
                  pl.BlockSpec(pool_w.shape, lambda b, t: (0, 0, 0)),
                  pl.BlockSpec((1, width), lambda b, t: (0, 0))],
        out_specs=[pl.BlockSpec((tt, width), lambda b, t: (b * nt + t, 0)),
                   pl.BlockSpec((None, POOL_HALO, width), lambda b, t: (b, 0, 0))],
        out_shape=[jax.ShapeDtypeStruct((m, width), BF16),
                   jax.ShapeDtypeStruct((batch, POOL_HALO, width), F32)],
        scratch_shapes=[pltpu.VMEM((tt + POOL_HALO, width), F32)],
        compiler_params=_params("parallel", "arbitrary"),
        name="pool_prompt",
    )(z, pool_w, pool_scale)


def _pool_sample_kernel(a_ref, buf_ref, pw_ref, ps_ref, ya_in_ref, o_ref, nbuf_ref, *, width, gd):
    del ya_in_ref
    a = a_ref[...]
    for gi, w in enumerate(POOL_WINDOWS):
        sl = slice(gi * gd, (gi + 1) * gd)
        acc = a[:, sl]
        for s in range(1, w):
            r = POOL_BUF - s
            acc = acc + buf_ref[:, r * width + gi * gd:r * width + (gi + 1) * gd]
        d = acc / float(min(PAST_LEN + 1, w)) - a[:, sl]
        y = _dot(d.astype(BF16), pw_ref[gi])
        o_ref[:, sl] = (y * ps_ref[:, sl]).astype(o_ref.dtype)
    nbuf_ref[:, 0:(POOL_BUF - 1) * width] = buf_ref[:, width:POOL_BUF * width]
    nbuf_ref[:, (POOL_BUF - 1) * width:POOL_BUF * width] = a


def _pool_sample(z, buf_flat, layer, pool_w, pool_scale, ya, *, row0, nsamp, width):
    ns = SAMPLE_BLOCK
    gd = width // len(POOL_WINDOWS)
    rb = row0 // ns
    ya, nbuf = pl.pallas_call(
        functools.partial(_pool_sample_kernel, width=width, gd=gd),
        grid=(nsamp // ns,),
        in_specs=[pl.BlockSpec((ns, width), lambda i: (rb + i, 0)),
                  pl.BlockSpec((None, ns, POOL_BUF * width), lambda i: (layer, i, 0)),
                  pl.BlockSpec(pool_w.shape, lambda i: (0, 0, 0)),
                  pl.BlockSpec((1, width), lambda i: (0, 0)),
                  pl.BlockSpec(memory_space=pl.ANY)],
        out_specs=[pl.BlockSpec((ns, width), lambda i: (rb + i, 0)),
                   pl.BlockSpec((ns, POOL_BUF * width), lambda i: (i, 0))],
        out_shape=[jax.ShapeDtypeStruct(ya.shape, ya.dtype),
                   jax.ShapeDtypeStruct(buf_flat.shape[1:], F32)],
        input_output_aliases={4: 0},
        compiler_params=_params("parallel"),
        name="pool_sample",
    )(z, buf_flat, pool_w, pool_scale, ya)
    return ya, nbuf


def _row_layer_norm(v, g, b):
    mu = jnp.mean(v, -1, keepdims=True)
    d = v - mu
    var = jnp.mean(d * d, -1, keepdims=True)
    return d * lax.rsqrt(var + EPS) * g + b


def _sgu_prompt_kernel(u_ref, v_ref, lng_ref, lnb_ref, ws_ref, biast_ref, o_ref, *, nchunk, gd):
    c = SGU_CHUNK
    tril = (lax.broadcasted_iota(jnp.int32, (c, c), 0) >= lax.broadcasted_iota(jnp.int32, (c, c), 1)).astype(F32)
    w_m = [(ws_ref[g] * tril).astype(BF16) for g in range(SGU_GROUPS)]
    for ci in range(nchunk):
        rows = slice(ci * c, (ci + 1) * c)
        vn = _row_layer_norm(v_ref[rows, :], lng_ref[...], lnb_ref[...]).astype(BF16)
        for g in range(SGU_GROUPS):
            sl = slice(g * gd, (g + 1) * gd)
            mixed = _dot(w_m[g], vn[:, sl]) + biast_ref[:, g:g + 1]
            o_ref[rows, sl] = (u_ref[rows, sl] * mixed).astype(o_ref.dtype)


def _sgu_prompt(z, ln_g, ln_b, ws, bias_t, *, batch, seq, m, width, u_blk, v_blk):
    assert seq % SGU_CHUNK == 0
    rows = _tile(seq, 512, SGU_CHUNK)
    nt = seq // rows
    gd = width // SGU_GROUPS
    return pl.pallas_call(
        functools.partial(_sgu_prompt_kernel, nchunk=rows // SGU_CHUNK, gd=gd),
        grid=(batch * nt,),
        in_specs=[pl.BlockSpec((rows, width), lambda i: (i, u_blk)),
                  pl.BlockSpec((rows, width), lambda i: (i, v_blk)),
                  pl.BlockSpec((1, width), lambda i: (0, 0)),
                  pl.BlockSpec((1, width), lambda i: (0, 0)),
                  pl.BlockSpec(ws.shape, lambda i: (0, 0, 0)),
                  pl.BlockSpec(bias_t.shape, lambda i: (0, 0))],
        out_specs=pl.BlockSpec((rows, width), lambda i: (i, 0)),
        out_shape=jax.ShapeDtypeStruct((m, width), BF16),
        compiler_params=_params("parallel"),
        name="sgu_prompt",
    )(z, z, ln_g, ln_b, ws, bias_t)


def _sgu_sample_kernel(u_ref, v_ref, lng_ref, lnb_ref, w0_ref, b0_ref, yb_in_ref, o_ref, vn_ref):
    del yb_in_ref
    vn = _row_layer_norm(v_ref[...], lng_ref[...], lnb_ref[...])
    vn_ref[...] = vn
    mixed = _round_bf16(w0_ref[...]) * _round_bf16(vn) + b0_ref[...]
    o_ref[...] = (u_ref[...] * mixed).astype(o_ref.dtype)


def _sgu_sample(z, ln_g, ln_b, w0, b0, yb, *, row0, nsamp, width, u_blk, v_blk):
    ns = SAMPLE_BLOCK
    rb = row0 // ns
    vec = pl.BlockSpec((1, width), lambda i: (0, 0))
    yb, vn = pl.pallas_call(
        _sgu_sample_kernel,
        grid=(nsamp // ns,),
        in_specs=[pl.BlockSpec((ns, width), lambda i: (rb + i, u_blk)),
                  pl.BlockSpec((ns, width), lambda i: (rb + i, v_blk)),
                  vec, vec, vec, vec,
                  pl.BlockSpec(memory_space=pl.ANY)],
        out_specs=[pl.BlockSpec((ns, width), lambda i: (rb + i, 0)),
                   pl.BlockSpec((ns, width), lambda i: (i, 0))],
        out_shape=[jax.ShapeDtypeStruct(yb.shape, yb.dtype),
                   jax.ShapeDtypeStruct((nsamp, width), F32)],
        input_output_aliases={6: 0},
        compiler_params=_params("parallel"),
        name="sgu_sample",
    )(z, z, ln_g, ln_b, w0, b0, yb)
    return yb, vn


def _cumsum_rows(x):
    rows = x.shape[0]
    row = lax.broadcasted_iota(jnp.int32, x.shape, 0)
    s = 1
    while s < rows:
        x = x + jnp.where(row >= s, pltpu.roll(x, s, 0), 0.0)
        s *= 2
    return x


def _gla_out(o, norm_g, r):
    o = o * lax.rsqrt(jnp.mean(o * o, -1, keepdims=True) + EPS) * norm_g
    return o * _silu(r)


def _gla_prompt_kernel(q_ref, k_ref, v01_ref, v23_ref, g_ref, r01_ref, r23_ref, ng_ref, o_ref, s_ref, st_ref, *,
                       nchunk, scale, hk, hv):
    t = pl.program_id(1)
    c = GLA_CHUNK

    @pl.when(t == 0)
    def _():
        st_ref[...] = jnp.zeros(st_ref.shape, F32)

    causal = (lax.broadcasted_iota(jnp.int32, (c, c), 0) >= lax.broadcasted_iota(jnp.int32, (c, c), 1)).astype(F32)
    v_refs = (v01_ref, v23_ref)
    r_refs = (r01_ref, r23_ref)
    for ci in range(nchunk):
        rows = slice(ci * c, (ci + 1) * c)
        for h in range(GLA_HEADS):
            kcols = slice(h * hk, (h + 1) * hk)
            vcols = slice((h % 2) * hv, (h % 2 + 1) * hv)
            ocols = slice(h * hv, (h + 1) * hv)
            b = _cumsum_rows(g_ref[rows, kcols])
            b_end = b[c - 1:c, :]
            k = k_ref[rows, kcols]
            q_in = ((q_ref[rows, kcols] * scale) * jnp.exp(b)).astype(BF16)
            k_in = (k * jnp.exp(-b)).astype(BF16)
            k_dec = (k * jnp.exp(b_end - b)).astype(BF16)
            vb = v_refs[h // 2][rows, vcols].astype(BF16)
            att = _dot_nt(q_in, k_in) * causal
            st = st_ref[h]
            o = _dot(att.astype(BF16), vb) + _dot_nt(q_in, st.astype(BF16))
            st_ref[h] = st * jnp.exp(b_end) + _dot_tn(vb, k_dec)
            o_ref[rows, ocols] = _gla_out(o, ng_ref[:, ocols], r_refs[h // 2][rows, vcols]).astype(o_ref.dtype)

    @pl.when(t == pl.num_programs(1) - 1)
    def _():
        for h in range(GLA_HEADS):
            s_ref[0, h] = st_ref[h].T


def _gla_prompt(z, logf, norm_g, *, batch, seq, m, hk, hv, q_off):
    assert seq % GLA_CHUNK == 0 and GLA_HEADS == 4
    rows = _tile(seq, 256, GLA_CHUNK)
    nt = seq // rows
    heads = GLA_HEADS
    dk, dv2 = heads * hk, 2 * hv
    v_off = q_off + 2 * dk
    assert q_off % dk == 0 and v_off % dv2 == 0

    def cols(width, off):
        return pl.BlockSpec((rows, width), lambda n, t: (n * nt + t, off // width))

    yc, s = pl.pallas_call(
        functools.partial(_gla_prompt_kernel, nchunk=rows // GLA_CHUNK, scale=hk ** -0.5, hk=hk, hv=hv),
        grid=(batch, nt),
        in_specs=[cols(dk, q_off), cols(dk, q_off + dk),
                  cols(dv2, v_off), cols(dv2, v_off + dv2),
                  cols(dk, 0),
                  cols(dv2, v_off + 2 * dv2), cols(dv2, v_off + 3 * dv2),
                  pl.BlockSpec((1, heads * hv), lambda n, t: (0, 0))],
        out_specs=[pl.BlockSpec((rows, heads * hv), lambda n, t: (n * nt + t, 0)),
                   pl.BlockSpec((1, heads, hk, hv), lambda n, t: (n, 0, 0, 0))],
        out_shape=[jax.ShapeDtypeStruct((m, heads * hv), BF16),
                   jax.ShapeDtypeStruct((batch, heads, hk, hv), F32)],
        scratch_shapes=[pltpu.VMEM((heads, hv, hk), F32)],
        compiler_params=_params("parallel", "arbitrary"),
        name="gla_prompt",
    )(z, z, z, z, logf, z, z, norm_g)
    return yc, s


def _gla_sample_kernel(*refs, scale):
    q_ref, k_ref, v_ref, g_ref, r_ref, ng_ref, s_ref, yc_in_ref = refs[:8]
    o_ref, so_ref = refs[-2:]
    del yc_in_ref
    ns = SAMPLE_BLOCK
    g = g_ref[...]
    e = jnp.exp(g)
    k = k_ref[...]
    q_in = _round_bf16((q_ref[...] * scale) * e)
    k_in = _round_bf16(k * jnp.exp(-g))
    k_dec = _round_bf16(k * jnp.exp(g - g))
    vb = _round_bf16(v_ref[...])
    att = jnp.sum(q_in * k_in, -1, keepdims=True)
    hk = g.shape[1]
    cols = jnp.concatenate([e, k_dec, q_in, jnp.zeros((128 - 3 * ns, hk), F32)], axis=0).T
    outs = []
    for j in range(ns):
        s = s_ref[0, j, 0]
        so_ref[0, j, 0] = s * cols[:, j:j + 1] + cols[:, ns + j:ns + j + 1] * vb[j:j + 1, :]
        outs.append(jnp.sum(cols[:, 2 * ns + j:2 * ns + j + 1] * _round_bf16(s), axis=0, keepdims=True))
    o = jnp.concatenate(outs, axis=0) + _round_bf16(att) * vb
    o_ref[...] = _gla_out(o, ng_ref[...], r_ref[...]).astype(o_ref.dtype)


def _gla_sample(z, logf, norm_g, state, layer, yc, prev_out, *, row0, nsamp, hk, hv, q_blk, k_blk, v_blk, r_blk):
    ns = SAMPLE_BLOCK
    rb = row0 // ns
    heads = GLA_HEADS
    st_spec = pl.BlockSpec((1, ns, 1, hk, hv), lambda i, h: (layer, i, h, 0, 0))
    in_specs = [pl.BlockSpec((ns, hk), lambda i, h: (rb + i, q_blk + h)),
                pl.BlockSpec((ns, hk), lambda i, h: (rb + i, k_blk + h)),
                pl.BlockSpec((ns, hv), lambda i, h: (rb + i, v_blk + h)),
                pl.BlockSpec((ns, hk), lambda i, h: (rb + i, h)),
                pl.BlockSpec((ns, hv), lambda i, h: (rb + i, r_blk + h)),
                pl.BlockSpec((1, hv), lambda i, h: (0, h)),
                st_spec,
                pl.BlockSpec(memory_space=pl.ANY)]
    args = [z, z, z, logf, z, norm_g, state, yc]
    aliases = {7: 0}
    if prev_out is not None:
        in_specs.append(pl.BlockSpec(memory_space=pl.ANY))
        args.append(prev_out)
        aliases[8] = 1
    yc, s_out = pl.pallas_call(
        functools.partial(_gla_sample_kernel, scale=hk ** -0.5),
        grid=(nsamp // ns, heads),
        in_specs=in_specs,
        out_specs=[pl.BlockSpec((ns, hv), lambda i, h: (rb + i, h)), st_spec],
        out_shape=[jax.ShapeDtypeStruct(yc.shape, yc.dtype),
                   jax.ShapeDtypeStruct(state.shape, F32)],
        input_output_aliases=aliases,
        compiler_params=_params("parallel", "arbitrary"),
        name="gla_sample",
    )(*args)
    return yc, s_out


def _ffn(x, xb, w1, w3, w2, layer, g, b, alpha):
    h = _ffn_up(xb, w1, w3, layer)
    y = _matmul(h, w2, layer, tm_target=1040, tn_target=256, name="ffn_down")
    ln = functools.partial(_res_ln, g=g, b=b, alpha=alpha, yscale=0.5)
    if not isinstance(x, tuple):
        return ln(x, y)
    x_p, x_s = x
    mp, m = x_p.shape[0], y.shape[0]
    outs = ln(x_p, y, out_rows=m)
    return ln(x_s, y, y_row0=mp, out_row0=mp, out_rows=m, into=outs)


def kernel(x_prompt, x_sample, state_pool, state_gla, p_prompt, p_sample, ffa_w1, ffa_w3, ffa_w2, ffb_w1, ffb_w3, ffb_w2, ln_g, ln_b, w_in, b_in, pool_w, pool_scale, sgu_ln_g, sgu_ln_b, sgu_ws, sgu_bias, gla_wa2, gla_ba, gla_norm_g, w_up_a, w_up_b, w_up_c, w_o, pe_w, pe_gate_w):
    depth = ffa_w1.shape[0]
    batch, seq, d = x_prompt.shape
    nsamp, dec_seq, _ = x_sample.shape
    assert dec_seq == 1 and nsamp % SAMPLE_BLOCK == 0
    mp = batch * seq
    m = mp + nsamp
    assert mp % SAMPLE_BLOCK == 0
    alpha = (2.0 * depth) ** 0.25

    pool_wd = pool_w.shape[1] * pool_w.shape[2]
    sgu_wd = sgu_ln_g.shape[1]
    dk = gla_wa2.shape[2]
    dv = gla_norm_g.shape[1]
    hk, hv = dk // GLA_HEADS, dv // GLA_HEADS
    n_main = pool_wd + 2 * sgu_wd + 2 * dk + 2 * dv
    n_gate = n_main + GLA_RANK
    assert w_in.shape[2] == n_gate + 3 * d
    assert pool_wd == sgu_wd
    u_blk, v_blk = pool_wd // sgu_wd, pool_wd // sgu_wd + 1
    q_off = pool_wd + 2 * sgu_wd
    q_blk, k_blk = q_off // hk, (q_off + dk) // hk
    v2_blk, r_blk = (q_off + 2 * dk) // hv, (q_off + 2 * dk + dv) // hv

    x_p2, x_s2 = x_prompt.reshape(mp, d), x_sample.reshape(nsamp, d)
    x = (x_p2, x_s2)
    xb = jnp.concatenate([x_p2, x_s2], axis=0).astype(BF16)
    w_in_t = jnp.swapaxes(w_in, 1, 2)

    pool_p, gla_p, pool_s, sgu_s = [], [], [], []
    gla_s = None
    state_pool_flat = state_pool.reshape(depth, nsamp, POOL_BUF * pool_wd)
    for i in range(depth):
        x, xb = _ffn(x, xb, ffa_w1, ffa_w3, ffa_w2, i, ln_g[i, 0], ln_b[i, 0], alpha)

        b_in_i = b_in[i].reshape(1, -1)
        z = _in_proj(xb, w_in_t, b_in_i, i, n_main)
        logf = _gla_logf(xb, w_in_t, b_in_i, i, n_main,
                         jnp.pad(gla_wa2[i].astype(BF16), ((0, 128 - GLA_RANK), (0, 0))),
                         gla_ba[i].reshape(1, dk))

        pw = pool_w[i].astype(BF16)
        ps = pool_scale[i].reshape(1, pool_wd)
        ya, tail = _pool_prompt(z, pw, ps, batch=batch, seq=seq, m=m, width=pool_wd)
        ya, nbuf = _pool_sample(z, state_pool_flat, i, pw, ps, ya, row0=mp, nsamp=nsamp, width=pool_wd)
        pool_p.append(tail[:, POOL_HALO - POOL_BUF:])
        pool_s.append(nbuf.reshape(nsamp, POOL_BUF, pool_wd))

        lng = sgu_ln_g[i].reshape(1, sgu_wd)
        lnb = sgu_ln_b[i].reshape(1, sgu_wd)
        gd = sgu_wd // SGU_GROUPS
        yb = _sgu_prompt(z, lng, lnb, sgu_ws[i], sgu_bias[i].T, batch=batch, seq=seq, m=m, width=sgu_wd,
                         u_blk=u_blk, v_blk=v_blk)
        yb, vn = _sgu_sample(z, lng, lnb,
                             jnp.repeat(sgu_ws[i, :, 0, 0], gd).reshape(1, sgu_wd),
                             jnp.repeat(sgu_bias[i, :, 0], gd).reshape(1, sgu_wd),
                             yb, row0=mp, nsamp=nsamp, width=sgu_wd, u_blk=u_blk, v_blk=v_blk)
        sgu_s.append(vn.reshape(nsamp, 1, sgu_wd))

        ng = gla_norm_g[i].reshape(1, dv)
        blks = dict(hk=hk, hv=hv, q_blk=q_blk, k_blk=k_blk, v_blk=v2_blk, r_blk=r_blk)
        yc, s_p = _gla_prompt(z, logf, ng, batch=batch, seq=seq, m=m, hk=hk, hv=hv, q_off=q_off)
        yc, gla_s = _gla_sample(z, logf, ng, state_gla, i, yc, gla_s, row0=mp, nsamp=nsamp, **blks)
        gla_p.append(s_p)

        w_g = _cast_rows(w_in_t, i, n_gate, 3 * d)
        h = _up_gate(xb, ya, yb, yc, w_g, b_in_i[:, n_gate:], w_up_a, w_up_b, w_up_c, i)
        y = _matmul(h, w_o, i, tm_target=1664, tn_target=512, name="out_proj")
        x, xb = _res_ln(x, y, ln_g[i, 1], ln_b[i, 1], alpha=alpha, yscale=None)

        x, xb = _ffn(x, xb, ffb_w1, ffb_w3, ffb_w2, i, ln_g[i, 2], ln_b[i, 2], alpha)

        pb = jnp.concatenate([p_prompt[i].reshape(mp, -1), p_sample[i].reshape(nsamp, -1)], axis=0).astype(BF16)
        y = _pe_embed(xb, pb, pe_gate_w, pe_w, i)
        ln = functools.partial(_res_ln, x, y, ln_g[i, 3], ln_b[i, 3], alpha=alpha, yscale=None)
        if i + 1 < depth:
            x, xb = ln()
        else:
            (x_p,) = ln(rows=mp, with_bf16=False)
            (x_s,) = ln(rows=nsamp, x_row0=mp, y_row0=mp, with_bf16=False)

    return (x_p.reshape(batch, seq, d), x_s.reshape(nsamp, 1, d),
            jnp.stack(pool_p), jnp.stack(gla_p).astype(state_gla.dtype),
            jnp.stack(pool_s), gla_s.astype(state_gla.dtype), jnp.stack(sgu_s))
```

```python
import functools

import jax
import jax.numpy as jnp
from jax import lax
from jax.experimental import pallas as pl
from jax.experimental.pallas import tpu as pltpu

F32 = jnp.float32
BF16 = jnp.bfloat16

POOL_WINDOWS = (2, 4, 8, 16)
POOL_BUF = max(POOL_WINDOWS) - 1
POOL_HALO = 16
SGU_GROUPS = 8
SGU_CHUNK = 128
GLA_HEADS = 4
GLA_RANK = 16
GLA_TAU = 16.0
GLA_CHUNK = 64
EPS = 1e-5
PAST_LEN = 16384
SAMPLE_BLOCK = 8

V7X_VMEM_LIMIT_BYTES = 60 * 1024 * 1024


def _tile(dim, target, align):
    best = None
    t = align
    while t <= min(dim, target):
        if dim % t == 0:
            best = t
        t += align
    return dim if best is None else best


def _params(*sem):
    return pltpu.CompilerParams(dimension_semantics=sem, vmem_limit_bytes=V7X_VMEM_LIMIT_BYTES)


def _silu(x):
    return x * jax.nn.sigmoid(x)


def _dot(a, b):
    return jnp.dot(a, b, preferred_element_type=F32)


def _dot_nt(a, b):
    return lax.dot_general(a, b, (((1,), (1,)), ((), ())), preferred_element_type=F32)


def _dot_tn(a, b):
    return lax.dot_general(a, b, (((0,), (0,)), ((), ())), preferred_element_type=F32)


def _round_bf16(x):
    return x.astype(BF16).astype(F32)


def _row_layer_norm(v, g, b):
    mu = jnp.mean(v, -1, keepdims=True)
    d = v - mu
    var = jnp.mean(d * d, -1, keepdims=True)
    return d * lax.rsqrt(var + EPS) * g + b


def _residual(x, y, alpha, yscale):
    return alpha * x + (y if yscale is None else yscale * y)


def _w_spec(w, layer, k, tn):
    assert w.ndim == 3 and w.shape[1] == k
    return pl.BlockSpec((None, k, tn), lambda i, j: (layer, 0, j))


def _resident_rows(tm, k):
    return pl.BlockSpec((tm, k), lambda i, j: (i, 0), pipeline_mode=pl.Buffered(1))


def _glu_kernel(x_ref, w1_ref, w3_ref, o_ref):
    x = x_ref[...]
    a = _dot(x, w1_ref[...].astype(BF16))
    b = _dot(x, w3_ref[...].astype(BF16))
    o_ref[...] = (_silu(a) * b).astype(o_ref.dtype)


def _ffn_up(xb, w1, w3, layer):
    m, k = xb.shape
    n = w1.shape[-1]
    tm = _tile(m, 2080, 16)
    tn = _tile(n, 256, 128)
    return pl.pallas_call(
        _glu_kernel,
        grid=(m // tm, n // tn),
        in_specs=[_resident_rows(tm, k),
                  _w_spec(w1, layer, k, tn),
                  _w_spec(w3, layer, k, tn)],
        out_specs=pl.BlockSpec((tm, tn), lambda i, j: (i, j)),
        out_shape=jax.ShapeDtypeStruct((m, n), BF16),
        compiler_params=_params("parallel", "arbitrary"),
        name="ffn_up",
    )(xb, w1, w3)


def _mm_kernel(x_ref, w_ref, o_ref):
    o_ref[...] = _dot(x_ref[...], w_ref[...].astype(BF16)).astype(o_ref.dtype)


def _mm_nt_bias_kernel(x_ref, wt_ref, b_ref, o_ref):
    o_ref[...] = (_dot_nt(x_ref[...], wt_ref[...].astype(BF16)) + b_ref[...]).astype(o_ref.dtype)


def _cast_kernel(w_ref, o_ref):
    o_ref[...] = w_ref[...].astype(o_ref.dtype)


def _cast_rows(w, layer, row0, rows):
    _, _, k = w.shape
    tr = _tile(rows, 512, 16)
    assert row0 % 8 == 0
    return pl.pallas_call(
        _cast_kernel,
        grid=(rows // tr,),
        in_specs=[pl.BlockSpec((None, pl.Element(tr), pl.Element(k)),
                               lambda r: (layer, pl.multiple_of(row0 + r * tr, 8), 0))],
        out_specs=pl.BlockSpec((tr, k), lambda r: (r, 0)),
        out_shape=jax.ShapeDtypeStruct((rows, k), BF16),
        compiler_params=_params("parallel"),
        name="cast_rows",
    )(w)


def _mm_res_kernel(x_ref, w_ref, r_ref, o_ref, *, alpha, yscale):
    o_ref[...] = _residual(r_ref[...], _dot(x_ref[...], w_ref[...].astype(BF16)), alpha, yscale)


def _matmul(xb, w, layer, *, tm_target, tn_target, name, residual=None, alpha=None, yscale=None):
    m, k = xb.shape
    n = w.shape[2]
    tm = _tile(m, tm_target, 16)
    tn = _tile(n, tn_target, 128)
    tile = pl.BlockSpec((tm, tn), lambda i, j: (i, j))
    in_specs = [_resident_rows(tm, k), _w_spec(w, layer, k, tn)]
    args = [xb, w]
    kern = _mm_kernel
    if residual is not None:
        in_specs.append(tile)
        args.append(residual)
        kern = functools.partial(_mm_res_kernel, alpha=alpha, yscale=yscale)
    return pl.pallas_call(
        kern,
        grid=(m // tm, n // tn),
        in_specs=in_specs,
        out_specs=tile,
        out_shape=jax.ShapeDtypeStruct((m, n), F32),
        compiler_params=_params("parallel", "arbitrary"),
        name=name,
    )(*args)


def _in_proj(xb, w_t, bias, layer, n):
    m, k = xb.shape
    tm = _tile(m, 1664, 16)
    tn = _tile(n, 512, 128)
    return pl.pallas_call(
        _mm_nt_bias_kernel,
        grid=(m // tm, n // tn),
        in_specs=[_resident_rows(tm, k),
                  pl.BlockSpec((None, tn, k), lambda i, j: (layer, j, 0)),
                  pl.BlockSpec((1, tn), lambda i, j: (0, j))],
        out_specs=pl.BlockSpec((tm, tn), lambda i, j: (i, j)),
        out_shape=jax.ShapeDtypeStruct((m, n), F32),
        compiler_params=_params("parallel", "arbitrary"),
        name="in_proj",
    )(xb, w_t, bias)


def _ln_kernel(*refs, alpha, yscale, n_src, n_alias):
    srcs, (g_ref, b_ref), outs = refs[:n_src], refs[n_src:n_src + 2], refs[n_src + 2 + n_alias:]
    t = srcs[0][...] if n_src == 1 else _residual(srcs[0][...], srcs[1][...], alpha, yscale)
    o = _row_layer_norm(t, g_ref[...], b_ref[...])
    outs[0][...] = o
    for ob_ref in outs[1:]:
        ob_ref[...] = o.astype(BF16)


def _res_ln(x, y, g, b, *, alpha=None, yscale=None, rows=None, x_row0=0, y_row0=0, out_row0=0, out_rows=None,
            with_bf16=True, into=None):
    d = x.shape[1]
    rows = x.shape[0] if rows is None else rows
    out_rows = rows if out_rows is None else out_rows
    tm = _tile(rows, 416, 16)
    assert x_row0 % tm == 0 and y_row0 % tm == 0 and out_row0 % tm == 0

    def at(row0):
        return pl.BlockSpec((tm, d), lambda i: (row0 // tm + i, 0))

    vec = pl.BlockSpec((1, d), lambda i: (0, 0))
    out_shape = [jax.ShapeDtypeStruct((out_rows, d), F32)]
    if with_bf16:
        out_shape.append(jax.ShapeDtypeStruct((out_rows, d), BF16))
    srcs = [(x, x_row0)] + ([] if y is None else [(y, y_row0)])
    in_specs = [at(r0) for _, r0 in srcs] + [vec, vec]
    args = [a for a, _ in srcs] + [g.reshape(1, d), b.reshape(1, d)]
    aliases = {}
    if into is not None:
        in_specs += [pl.BlockSpec(memory_space=pl.ANY)] * len(into)
        aliases = {len(args) + n: n for n in range(len(into))}
        args += list(into)
    return pl.pallas_call(
        functools.partial(_ln_kernel, alpha=alpha, yscale=yscale, n_src=len(srcs), n_alias=len(aliases)),
        grid=(rows // tm,),
        in_specs=in_specs,
        out_specs=[at(out_row0)] * len(out_shape),
        out_shape=out_shape,
        input_output_aliases=aliases,
        compiler_params=_params("parallel"),
        name="res_ln",
    )(*args)


def _log_sigmoid(x):
    return jnp.minimum(x, 0.0) - jnp.log1p(jnp.exp(-jnp.abs(x)))


def _logf_kernel(x_ref, wt_ref, bl_ref, wa_ref, ba_ref, o_ref):
    a_lr = _dot_nt(x_ref[...], wt_ref[...].astype(BF16)) + bl_ref[...]
    t = _dot(a_lr.astype(BF16), wa_ref[...]) + ba_ref[...]
    o_ref[...] = _log_sigmoid(t) / GLA_TAU


def _gla_logf(xb, w_in_t, b_in, layer, col0, wa2_pad, ba):
    m, k = xb.shape
    lane, dk = wa2_pad.shape
    assert col0 % lane == 0
    tm = _tile(m, 1040, 16)
    return pl.pallas_call(
        _logf_kernel,
        grid=(m // tm,),
        in_specs=[pl.BlockSpec((tm, k), lambda i: (i, 0)),
                  pl.BlockSpec((None, lane, k), lambda i: (layer, col0 // lane, 0)),
                  pl.BlockSpec((1, lane), lambda i: (0, col0 // lane)),
                  pl.BlockSpec((lane, dk), lambda i: (0, 0)),
                  pl.BlockSpec((1, dk), lambda i: (0, 0))],
        out_specs=pl.BlockSpec((tm, dk), lambda i: (i, 0)),
        out_shape=jax.ShapeDtypeStruct((m, dk), F32),
        compiler_params=_params("parallel"),
        name="gla_logf",
    )(xb, w_in_t, b_in, wa2_pad, ba)


def _upgate_kernel(x_ref, ya_ref, yb_ref, yc_ref, wga_ref, wgb_ref, wgc_ref, bga_ref, bgb_ref, bgc_ref,
                   wa_ref, wb_ref, wc_ref, o_ref):
    x = x_ref[...]
    acc = jax.nn.sigmoid(_dot_nt(x, wga_ref[...]) + bga_ref[...]) * _dot(ya_ref[...], wa_ref[...].astype(BF16))
    acc = acc + jax.nn.sigmoid(_dot_nt(x, wgb_ref[...]) + bgb_ref[...]) * _dot(yb_ref[...], wb_ref[...].astype(BF16))
    acc = acc + jax.nn.sigmoid(_dot_nt(x, wgc_ref[...]) + bgc_ref[...]) * _dot(yc_ref[...], wc_ref[...].astype(BF16))
    o_ref[...] = acc.astype(o_ref.dtype)


def _up_gate(xb, ya, yb, yc, w_g, b_g, w_up_a, w_up_b, w_up_c, layer):
    m, k = xb.shape
    d = w_up_a.shape[-1]
    tm = _tile(m, 1040, 16)
    tn = _tile(d, 256, 128)
    nb = d // tn

    def rows(width):
        return _resident_rows(tm, width)

    def gate_cols(branch):
        return pl.BlockSpec((tn, k), lambda i, j: (branch * nb + j, 0))

    def gate_bias(branch):
        return pl.BlockSpec((1, tn), lambda i, j: (0, branch * nb + j))

    def up_cols(width):
        return pl.BlockSpec((None, width, tn), lambda i, j: (layer, 0, j))

    return pl.pallas_call(
        _upgate_kernel,
        grid=(m // tm, nb),
        in_specs=[rows(k), rows(ya.shape[1]), rows(yb.shape[1]), rows(yc.shape[1]),
                  gate_cols(0), gate_cols(1), gate_cols(2),
                  gate_bias(0), gate_bias(1), gate_bias(2),
                  up_cols(ya.shape[1]), up_cols(yb.shape[1]), up_cols(yc.shape[1])],
        out_specs=pl.BlockSpec((tm, tn), lambda i, j: (i, j)),
        out_shape=jax.ShapeDtypeStruct((m, d), BF16),
        compiler_params=_params("parallel", "arbitrary"),
        name="up_gate",
    )(xb, ya, yb, yc, w_g, w_g, w_g, b_g, b_g, b_g, w_up_a, w_up_b, w_up_c)


def _pe_kernel(x_ref, p_ref, wg_ref, wp_ref, r_ref, o_ref, *, alpha):
    y = jax.nn.sigmoid(_dot(x_ref[...], wg_ref[...].astype(BF16))) * _dot(p_ref[...], wp_ref[...].astype(BF16))
    o_ref[...] = _residual(r_ref[...], y, alpha, None)


def _pe_embed(xb, pb, w_gate, w_p, layer, residual, alpha):
    m, k = xb.shape
    d = w_gate.shape[-1]
    kp = pb.shape[1]
    tm = _tile(m, 1664, 16)
    tn = _tile(d, 512, 128)
    tile = pl.BlockSpec((tm, tn), lambda i, j: (i, j))
    return pl.pallas_call(
        functools.partial(_pe_kernel, alpha=alpha),
        grid=(m // tm, d // tn),
        in_specs=[_resident_rows(tm, k),
                  _resident_rows(tm, kp),
                  _w_spec(w_gate, layer, k, tn),
                  _w_spec(w_p, layer, kp, tn),
                  tile],
        out_specs=tile,
        out_shape=jax.ShapeDtypeStruct((m, d), F32),
        compiler_params=_params("parallel", "arbitrary"),
        name="pe_embed",
    )(xb, pb, w_gate, w_p, residual)


def _pool_prompt_kernel(a_ref, pw_ref, ps_ref, o_ref, tail_ref, ext_ref, *, tt, gd):
    t = pl.program_id(1)

    @pl.when(t == pl.num_programs(1) - 1)
    def _():
        tail_ref[...] = a_ref[tt - POOL_HALO:tt, :]

    @pl.when(t == 0)
    def _():
        ext_ref[0:POOL_HALO, :] = jnp.zeros((POOL_HALO, ext_ref.shape[1]), F32)

    a = a_ref[...]
    ext_ref[POOL_HALO:POOL_HALO + tt, :] = a
    pos = t * tt + lax.broadcasted_iota(jnp.int32, (tt, 1), 0)
    for gi, w in enumerate(POOL_WINDOWS):
        sl = slice(gi * gd, (gi + 1) * gd)
        acc = a[:, sl]
        for s in range(1, w):
            acc = acc + ext_ref[POOL_HALO - s:POOL_HALO - s + tt, sl]
        cnt = jnp.minimum(pos + 1, w).astype(F32)
        d = acc / cnt - a[:, sl]
        y = _dot(d.astype(BF16), pw_ref[gi])
        o_ref[:, sl] = (y * ps_ref[:, sl]).astype(o_ref.dtype)
    ext_ref[0:POOL_HALO, :] = ext_ref[tt:tt + POOL_HALO, :]


def _pool_prompt(z, pool_w, pool_scale, *, batch, seq, m, width):
    tt = _tile(seq, 256, 16)
    nt = seq // tt
    gd = width // len(POOL_WINDOWS)
    return pl.pallas_call(
        functools.partial(_pool_prompt_kernel, tt=tt, gd=gd),
        grid=(batch, nt),
        in_specs=[pl.BlockSpec((tt, width), lambda b, t: (b * nt + t, 0)),
                  pl.BlockSpec(pool_w.shape, lambda b, t: (0, 0, 0)),
                  pl.BlockSpec((1, width), lambda b, t: (0, 0))],
        out_specs=[pl.BlockSpec((tt, width), lambda b, t: (b * nt + t, 0)),
                   pl.BlockSpec((None, POOL_HALO, width), lambda b, t: (b, 0, 0))],
        out_shape=[jax.ShapeDtypeStruct((m, width), BF16),
                   jax.ShapeDtypeStruct((batch, POOL_HALO, width), F32)],
        scratch_shapes=[pltpu.VMEM((tt + POOL_HALO, width), F32)],
        compiler_params=_params("parallel", "arbitrary"),
        name="pool_prompt",
    )(z, pool_w, pool_scale)


def _pool_sample_kernel(a_ref, buf_ref, pw_ref, ps_ref, ya_in_ref, o_ref, nbuf_ref, *, width, gd):
    del ya_in_ref
    a = a_ref[...]
    for gi, w in enumerate(POOL_WINDOWS):
        sl = slice(gi * gd, (gi + 1) * gd)
        acc = a[:, sl]
        for s in range(1, w):
            r = POOL_BUF - s
            acc = acc + buf_ref[:, r * width + gi * gd:r * width + (gi + 1) * gd]
        d = acc / float(min(PAST_LEN + 1, w)) - a[:, sl]
        y = _dot(d.astype(BF16), pw_ref[gi])
        o_ref[:, sl] = (y * ps_ref[:, sl]).astype(o_ref.dtype)
    nbuf_ref[:, 0:(POOL_BUF - 1) * width] = buf_ref[:, width:POOL_BUF * width]
    nbuf_ref[:, (POOL_BUF - 1) * width:POOL_BUF * width] = a


def _pool_sample(z, buf_flat, layer, pool_w, pool_scale, ya, *, row0, nsamp, width):
    ns = SAMPLE_BLOCK
    gd = width // len(POOL_WINDOWS)
    rb = row0 // ns
    ya, nbuf = pl.pallas_call(
        functools.partial(_pool_sample_kernel, width=width, gd=gd),
        grid=(nsamp // ns,),
        in_specs=[pl.BlockSpec((ns, width), lambda i: (rb + i, 0)),
                  pl.BlockSpec((None, ns, POOL_BUF * width), lambda i: (layer, i, 0)),
                  pl.BlockSpec(pool_w.shape, lambda i: (0, 0, 0)),
                  pl.BlockSpec((1, width), lambda i: (0, 0)),
                  pl.BlockSpec(memory_space=pl.ANY)],
        out_specs=[pl.BlockSpec((ns, width), lambda i: (rb + i, 0)),
                   pl.BlockSpec((ns, POOL_BUF * width), lambda i: (i, 0))],
        out_shape=[jax.ShapeDtypeStruct(ya.shape, ya.dtype),
                   jax.ShapeDtypeStruct(buf_flat.shape[1:], F32)],
        input_output_aliases={4: 0},
        compiler_params=_params("parallel"),
        name="pool_sample",
    )(z, buf_flat, pool_w, pool_scale, ya)
    return ya, nbuf


def _sgu_prompt_kernel(u_ref, v_ref, lng_ref, lnb_ref, ws_ref, biast_ref, o_ref, *, nchunk, gd):
    c = SGU_CHUNK
    tril = (lax.broadcasted_iota(jnp.int32, (c, c), 0) >= lax.broadcasted_iota(jnp.int32, (c, c), 1)).astype(F32)
    w_m = [(ws_ref[g] * tril).astype(BF16) for g in range(SGU_GROUPS)]
    for ci in range(nchunk):
        rows = slice(ci * c, (ci + 1) * c)
        vn = _row_layer_norm(v_ref[rows, :], lng_ref[...], lnb_ref[...]).astype(BF16)
        for g in range(SGU_GROUPS):
            sl = slice(g * gd, (g + 1) * gd)
            mixed = _dot(w_m[g], vn[:, sl]) + biast_ref[:, g:g + 1]
            o_ref[rows, sl] = (u_ref[rows, sl] * mixed).astype(o_ref.dtype)


def _sgu_prompt(z, ln_g, ln_b, ws, bias_t, *, batch, seq, m, width, u_blk, v_blk):
    assert seq % SGU_CHUNK == 0
    rows = _tile(seq, 512, SGU_CHUNK)
    nt = seq // rows
    gd = width // SGU_GROUPS
    return pl.pallas_call(
        functools.partial(_sgu_prompt_kernel, nchunk=rows // SGU_CHUNK, gd=gd),
        grid=(batch * nt,),
        in_specs=[pl.BlockSpec((rows, width), lambda i: (i, u_blk)),
                  pl.BlockSpec((rows, width), lambda i: (i, v_blk)),
                  pl.BlockSpec((1, width), lambda i: (0, 0)),
                  pl.BlockSpec((1, width), lambda i: (0, 0)),
                  pl.BlockSpec(ws.shape, lambda i: (0, 0, 0)),
                  pl.BlockSpec(bias_t.shape, lambda i: (0, 0))],
        out_specs=pl.BlockSpec((rows, width), lambda i: (i, 0)),
        out_shape=jax.ShapeDtypeStruct((m, width), BF16),
        compiler_params=_params("parallel"),
        name="sgu_prompt",
    )(z, z, ln_g, ln_b, ws, bias_t)


def _sgu_sample_kernel(u_ref, v_ref, lng_ref, lnb_ref, w0_ref, b0_ref, yb_in_ref, o_ref, vn_ref):
    del yb_in_ref
    vn = _row_layer_norm(v_ref[...], lng_ref[...], lnb_ref[...])
    vn_ref[...] = vn
    mixed = _round_bf16(w0_ref[...]) * _round_bf16(vn) + b0_ref[...]
    o_ref[...] = (u_ref[...] * mixed).astype(o_ref.dtype)


def _sgu_sample(z, ln_g, ln_b, w0, b0, yb, *, row0, nsamp, width, u_blk, v_blk):
    ns = SAMPLE_BLOCK
    rb = row0 // ns
    vec = pl.BlockSpec((1, width), lambda i: (0, 0))
    yb, vn = pl.pallas_call(
        _sgu_sample_kernel,
        grid=(nsamp // ns,),
        in_specs=[pl.BlockSpec((ns, width), lambda i: (rb + i, u_blk)),
                  pl.BlockSpec((ns, width), lambda i: (rb + i, v_blk)),
                  vec, vec, vec, vec,
                  pl.BlockSpec(memory_space=pl.ANY)],
        out_specs=[pl.BlockSpec((ns, width), lambda i: (rb + i, 0)),
                   pl.BlockSpec((ns, width), lambda i: (i, 0))],
        out_shape=[jax.ShapeDtypeStruct(yb.shape, yb.dtype),
                   jax.ShapeDtypeStruct((nsamp, width), F32)],
        input_output_aliases={6: 0},
        compiler_params=_params("parallel"),
        name="sgu_sample",
    )(z, z, ln_g, ln_b, w0, b0, yb)
    return yb, vn


def _cumsum_rows(x):
    rows = x.shape[0]
    row = lax.broadcasted_iota(jnp.int32, x.shape, 0)
    s = 1
    while s < rows:
        x = x + jnp.where(row >= s, pltpu.roll(x, s, 0), 0.0)
        s *= 2
    return x


def _gla_out(o, norm_g, r):
    o = o * lax.rsqrt(jnp.mean(o * o, -1, keepdims=True) + EPS) * norm_g
    return o * _silu(r)


def _gla_prompt_kernel(q_ref, k_ref, v01_ref, v23_ref, g_ref, r01_ref, r23_ref, ng_ref, o_ref, s_ref, st_ref, *,
                       nchunk, scale, hk, hv):
    t = pl.program_id(1)
    c = GLA_CHUNK

    @pl.when(t == 0)
    def _():
        st_ref[...] = jnp.zeros(st_ref.shape, F32)

    causal = (lax.broadcasted_iota(jnp.int32, (c, c), 0) >= lax.broadcasted_iota(jnp.int32, (c, c), 1)).astype(F32)
    v_refs = (v01_ref, v23_ref)
    r_refs = (r01_ref, r23_ref)
    for ci in range(nchunk):
        rows = slice(ci * c, (ci + 1) * c)
        for h in range(GLA_HEADS):
            kcols = slice(h * hk, (h + 1) * hk)
            vcols = slice((h % 2) * hv, (h % 2 + 1) * hv)
            ocols = slice(h * hv, (h + 1) * hv)
            b = _cumsum_rows(g_ref[rows, kcols])
            b_end = b[c - 1:c, :]
            k = k_ref[rows, kcols]
            q_in = ((q_ref[rows, kcols] * scale) * jnp.exp(b)).astype(BF16)
            k_in = (k * jnp.exp(-b)).astype(BF16)
            k_dec = (k * jnp.exp(b_end - b)).astype(BF16)
            vb = v_refs[h // 2][rows, vcols].astype(BF16)
            att = _dot_nt(q_in, k_in) * causal
            st = st_ref[h]
            o = _dot(att.astype(BF16), vb) + _dot_nt(q_in, st.astype(BF16))
            st_ref[h] = st * jnp.exp(b_end) + _dot_tn(vb, k_dec)
            o_ref[rows, ocols] = _gla_out(o, ng_ref[:, ocols], r_refs[h // 2][rows, vcols]).astype(o_ref.dtype)

    @pl.when(t == pl.num_programs(1) - 1)
    def _():
        for h in range(GLA_HEADS):
            s_ref[0, h] = st_ref[h].T


def _gla_prompt(z, logf, norm_g, *, batch, seq, m, hk, hv, q_off):
    assert seq % GLA_CHUNK == 0 and GLA_HEADS == 4
    rows = _tile(seq, 256, GLA_CHUNK)
    nt = seq // rows
    heads = GLA_HEADS
    dk, dv2 = heads * hk, 2 * hv
    v_off = q_off + 2 * dk
    assert q_off % dk == 0 and v_off % dv2 == 0

    def cols(width, off):
        return pl.BlockSpec((rows, width), lambda n, t: (n * nt + t, off // width))

    yc, s = pl.pallas_call(
        functools.partial(_gla_prompt_kernel, nchunk=rows // GLA_CHUNK, scale=hk ** -0.5, hk=hk, hv=hv),
        grid=(batch, nt),
        in_specs=[cols(dk, q_off), cols(dk, q_off + dk),
                  cols(dv2, v_off), cols(dv2, v_off + dv2),
                  cols(dk, 0),
                  cols(dv2, v_off + 2 * dv2), cols(dv2, v_off + 3 * dv2),
                  pl.BlockSpec((1, heads * hv), lambda n, t: (0, 0))],
        out_specs=[pl.BlockSpec((rows, heads * hv), lambda n, t: (n * nt + t, 0)),
                   pl.BlockSpec((1, heads, hk, hv), lambda n, t: (n, 0, 0, 0))],
        out_shape=[jax.ShapeDtypeStruct((m, heads * hv), BF16),
                   jax.ShapeDtypeStruct((batch, heads, hk, hv), F32)],
        scratch_shapes=[pltpu.VMEM((heads, hv, hk), F32)],
        compiler_params=_params("parallel", "arbitrary"),
        name="gla_prompt",
    )(z, z, z, z, logf, z, z, norm_g)
    return yc, s


def _gla_sample_kernel(*refs, scale):
    q_ref, k_ref, v_ref, g_ref, r_ref, ng_ref, s_ref, yc_in_ref = refs[:8]
    o_ref, so_ref = refs[-2:]
    del yc_in_ref
    ns = SAMPLE_BLOCK
    g = g_ref[...]
    e = jnp.exp(g)
    k = k_ref[...]
    q_in = _round_bf16((q_ref[...] * scale) * e)
    k_in = _round_bf16(k * jnp.exp(-g))
    k_dec = _round_bf16(k * jnp.exp(g - g))
    vb = _round_bf16(v_ref[...])
    att = jnp.sum(q_in * k_in, -1, keepdims=True)
    hk = g.shape[1]
    cols = jnp.concatenate([e, k_dec, q_in, jnp.zeros((128 - 3 * ns, hk), F32)], axis=0).T
    outs = []
    for j in range(ns):
        s = s_ref[0, j, 0]
        so_ref[0, j, 0] = s * cols[:, j:j + 1] + cols[:, ns + j:ns + j + 1] * vb[j:j + 1, :]
        outs.append(jnp.sum(cols[:, 2 * ns + j:2 * ns + j + 1] * _round_bf16(s), axis=0, keepdims=True))
    o = jnp.concatenate(outs, axis=0) + _round_bf16(att) * vb
    o_ref[...] = _gla_out(o, ng_ref[...], r_ref[...]).astype(o_ref.dtype)


def _gla_sample(z, logf, norm_g, state, layer, yc, prev_out, *, row0, nsamp, hk, hv, q_blk, k_blk, v_blk, r_blk):
    ns = SAMPLE_BLOCK
    rb = row0 // ns
    heads = GLA_HEADS
    st_spec = pl.BlockSpec((1, ns, 1, hk, hv), lambda i, h: (layer, i, h, 0, 0))
    in_specs = [pl.BlockSpec((ns, hk), lambda i, h: (rb + i, q_blk + h)),
                pl.BlockSpec((ns, hk), lambda i, h: (rb + i, k_blk + h)),
                pl.BlockSpec((ns, hv), lambda i, h: (rb + i, v_blk + h)),
                pl.BlockSpec((ns, hk), lambda i, h: (rb + i, h)),
                pl.BlockSpec((ns, hv), lambda i, h: (rb + i, r_blk + h)),
                pl.BlockSpec((1, hv), lambda i, h: (0, h)),
                st_spec,
                pl.BlockSpec(memory_space=pl.ANY)]
    args = [z, z, z, logf, z, norm_g, state, yc]
    aliases = {7: 0}
    if prev_out is not None:
        in_specs.append(pl.BlockSpec(memory_space=pl.ANY))
        args.append(prev_out)
        aliases[8] = 1
    yc, s_out = pl.pallas_call(
        functools.partial(_gla_sample_kernel, scale=hk ** -0.5),
        grid=(nsamp // ns, heads),
        in_specs=in_specs,
        out_specs=[pl.BlockSpec((ns, hv), lambda i, h: (rb + i, h)), st_spec],
        out_shape=[jax.ShapeDtypeStruct(yc.shape, yc.dtype),
                   jax.ShapeDtypeStruct(state.shape, F32)],
        input_output_aliases=aliases,
        compiler_params=_params("parallel", "arbitrary"),
        name="gla_sample",
    )(*args)
    return yc, s_out


def _ffn(x, xb, w1, w3, w2, layer, g, b, alpha):
    h = _ffn_up(xb, w1, w3, layer)
    down = functools.partial(_matmul, h, w2, layer, tm_target=1040, tn_target=256, name="ffn_down")
    if not isinstance(x, tuple):
        return _res_ln(down(residual=x, alpha=alpha, yscale=0.5), None, g, b)
    y = down()
    ln = functools.partial(_res_ln, g=g, b=b, alpha=alpha, yscale=0.5)
    x_p, x_s = x
    mp, m = x_p.shape[0], y.shape[0]
    outs = ln(x_p, y, out_rows=m)
    return ln(x_s, y, y_row0=mp, out_row0=mp, out_rows=m, into=outs)


def kernel(x_prompt, x_sample, state_pool, state_gla, p_prompt, p_sample, ffa_w1, ffa_w3, ffa_w2, ffb_w1, ffb_w3, ffb_w2, ln_g, ln_b, w_in, b_in, pool_w, pool_scale, sgu_ln_g, sgu_ln_b, sgu_ws, sgu_bias, gla_wa2, gla_ba, gla_norm_g, w_up_a, w_up_b, w_up_c, w_o, pe_w, pe_gate_w):
    depth = ffa_w1.shape[0]
    batch, seq, d = x_prompt.shape
    nsamp, dec_seq, _ = x_sample.shape
    assert dec_seq == 1 and nsamp % SAMPLE_BLOCK == 0
    mp = batch * seq
    m = mp + nsamp
    assert mp % SAMPLE_BLOCK == 0
    alpha = (2.0 * depth) ** 0.25

    pool_wd = pool_w.shape[1] * pool_w.shape[2]
    sgu_wd = sgu_ln_g.shape[1]
    dk = gla_wa2.shape[2]
    dv = gla_norm_g.shape[1]
    hk, hv = dk // GLA_HEADS, dv // GLA_HEADS
    n_main = pool_wd + 2 * sgu_wd + 2 * dk + 2 * dv
    n_gate = n_main + GLA_RANK
    assert w_in.shape[2] == n_gate + 3 * d
    assert pool_wd == sgu_wd
    u_blk, v_blk = pool_wd // sgu_wd, pool_wd // sgu_wd + 1
    q_off = pool_wd + 2 * sgu_wd
    q_blk, k_blk = q_off // hk, (q_off + dk) // hk
    v2_blk, r_blk = (q_off + 2 * dk) // hv, (q_off + 2 * dk + dv) // hv

    x_p2, x_s2 = x_prompt.reshape(mp, d), x_sample.reshape(nsamp, d)
    x = (x_p2, x_s2)
    xb = jnp.concatenate([x_p2, x_s2], axis=0).astype(BF16)
    w_in_t = jnp.swapaxes(w_in, 1, 2)

    pool_p, gla_p, pool_s, sgu_s = [], [], [], []
    gla_s = None
    state_pool_flat = state_pool.reshape(depth, nsamp, POOL_BUF * pool_wd)
    for i in range(depth):
        x, xb = _ffn(x, xb, ffa_w1, ffa_w3, ffa_w2, i, ln_g[i, 0], ln_b[i, 0], alpha)

        b_in_i = b_in[i].reshape(1, -1)
        z = _in_proj(xb, w_in_t, b_in_i, i, n_main)
        logf = _gla_logf(xb, w_in_t, b_in_i, i, n_main,
                         jnp.pad(gla_wa2[i].astype(BF16), ((0, 128 - GLA_RANK), (0, 0))),
                         gla_ba[i].reshape(1, dk))

        pw = pool_w[i].astype(BF16)
        ps = pool_scale[i].reshape(1, pool_wd)
        ya, tail = _pool_prompt(z, pw, ps, batch=batch, seq=seq, m=m, width=pool_wd)
        ya, nbuf = _pool_sample(z, state_pool_flat, i, pw, ps, ya, row0=mp, nsamp=nsamp, width=pool_wd)
        pool_p.append(tail[:, POOL_HALO - POOL_BUF:])
        pool_s.append(nbuf.reshape(nsamp, POOL_BUF, pool_wd))

        lng = sgu_ln_g[i].reshape(1, sgu_wd)
        lnb = sgu_ln_b[i].reshape(1, sgu_wd)
        gd = sgu_wd // SGU_GROUPS
        yb = _sgu_prompt(z, lng, lnb, sgu_ws[i], sgu_bias[i].T, batch=batch, seq=seq, m=m, width=sgu_wd,
                         u_blk=u_blk, v_blk=v_blk)
        yb, vn = _sgu_sample(z, lng, lnb,
                             jnp.repeat(sgu_ws[i, :, 0, 0], gd).reshape(1, sgu_wd),
                             jnp.repeat(sgu_bias[i, :, 0], gd).reshape(1, sgu_wd),
                             yb, row0=mp, nsamp=nsamp, width=sgu_wd, u_blk=u_blk, v_blk=v_blk)
        sgu_s.append(vn.reshape(nsamp, 1, sgu_wd))

        ng = gla_norm_g[i].reshape(1, dv)
        blks = dict(hk=hk, hv=hv, q_blk=q_blk, k_blk=k_blk, v_blk=v2_blk, r_blk=r_blk)
        yc, s_p = _gla_prompt(z, logf, ng, batch=batch, seq=seq, m=m, hk=hk, hv=hv, q_off=q_off)
        yc, gla_s = _gla_sample(z, logf, ng, state_gla, i, yc, gla_s, row0=mp, nsamp=nsamp, **blks)
        gla_p.append(s_p)

        w_g = _cast_rows(w_in_t, i, n_gate, 3 * d)
        h = _up_gate(xb, ya, yb, yc, w_g, b_in_i[:, n_gate:], w_up_a, w_up_b, w_up_c, i)
        t = _matmul(h, w_o, i, tm_target=1664, tn_target=512, name="out_proj", residual=x, alpha=alpha)
        x, xb = _res_ln(t, None, ln_g[i, 1], ln_b[i, 1])

        x, xb = _ffn(x, xb, ffb_w1, ffb_w3, ffb_w2, i, ln_g[i, 2], ln_b[i, 2], alpha)

        pb = jnp.concatenate([p_prompt[i].reshape(mp, -1), p_sample[i].reshape(nsamp, -1)], axis=0).astype(BF16)
        t = _pe_embed(xb, pb, pe_gate_w, pe_w, i, x, alpha)
        ln = functools.partial(_res_ln, t, None, ln_g[i, 3], ln_b[i, 3])
        if i + 1 < depth:
            x, xb = ln()
        else:
            (x_p,) = ln(rows=mp, with_bf16=False)
            (x_s,) = ln(rows=nsamp, x_row0=mp, with_bf16=False)

    return (x_p.reshape(batch, seq, d), x_s.reshape(nsamp, 1, d),
            jnp.stack(pool_p), jnp.stack(gla_p).astype(state_gla.dtype),
            jnp.stack(pool_s), gla_s.astype(state_gla.dtype), jnp.stack(sgu_s))
```

```python
import functools
from typing import NamedTuple

import jax
import jax.numpy as jnp
from jax import lax
from jax.experimental import pallas as pl
from jax.experimental.pallas import tpu as pltpu

F32 = jnp.float32
BF16 = jnp.bfloat16

POOL_WINDOWS = (2, 4, 8, 16)
POOL_BUF = max(POOL_WINDOWS) - 1
POOL_HALO = 16
SGU_GROUPS = 8
SGU_CHUNK = 128
GLA_HEADS = 4
GLA_RANK = 16
GLA_TAU = 16.0
GLA_CHUNK = 64
EPS = 1e-5
PAST_LEN = 16384
SAMPLE_BLOCK = 8

LANES = 128
V7X_VMEM_LIMIT_BYTES = 60 * 1024 * 1024


def _tile(dim, target, align):
    best = None
    t = align
    while t <= min(dim, target):
        if dim % t == 0:
            best = t
        t += align
    return dim if best is None else best


def _params(*sem):
    return pltpu.CompilerParams(dimension_semantics=sem, vmem_limit_bytes=V7X_VMEM_LIMIT_BYTES)


def _silu(x):
    return x * jax.nn.sigmoid(x)


def _dot(a, b):
    return jnp.dot(a, b, preferred_element_type=F32)


def _dot_nt(a, b):
    return lax.dot_general(a, b, (((1,), (1,)), ((), ())), preferred_element_type=F32)


def _dot_tn(a, b):
    return lax.dot_general(a, b, (((0,), (0,)), ((), ())), preferred_element_type=F32)


def _round_bf16(x):
    return x.astype(BF16).astype(F32)


def _row_stats(v):
    mu = jnp.mean(v, -1, keepdims=True)
    d = v - mu
    return mu, lax.rsqrt(jnp.mean(d * d, -1, keepdims=True) + EPS)


def _apply_norm(v, mu, rstd, g, b):
    return (v - mu) * rstd * g + b


def _row_layer_norm(v, g, b):
    mu, rstd = _row_stats(v)
    return _apply_norm(v, mu, rstd, g, b)


def _residual(x, y, alpha, yscale):
    return alpha * x + (y if yscale is None else yscale * y)


class _Normed(NamedTuple):
    t: jax.Array
    mu: jax.Array
    rstd: jax.Array
    g: jax.Array
    b: jax.Array


def _stream_specs(x, tm, tn):
    tile = pl.BlockSpec((tm, tn), lambda i, j: (i, j))
    if not isinstance(x, _Normed):
        return [tile], [x]
    stat = pl.BlockSpec((tm, LANES), lambda i, j: (i, 0))
    vec = pl.BlockSpec((1, tn), lambda i, j: (0, j))
    return [tile, stat, stat, vec, vec], list(x)


def _stream_tile(refs):
    if len(refs) == 1:
        return refs[0][...]
    t_ref, mu_ref, rstd_ref, g_ref, b_ref = refs
    return _apply_norm(t_ref[...], mu_ref[:, 0:1], rstd_ref[:, 0:1], g_ref[...], b_ref[...])


def _w_spec(w, layer, k, tn):
    assert w.ndim == 3 and w.shape[1] == k
    return pl.BlockSpec((None, k, tn), lambda i, j: (layer, 0, j))


def _resident_rows(tm, k):
    return pl.BlockSpec((tm, k), lambda i, j: (i, 0), pipeline_mode=pl.Buffered(1))


def _glu_kernel(x_ref, w1_ref, w3_ref, o_ref):
    x = x_ref[...]
    a = _dot(x, w1_ref[...].astype(BF16))
    b = _dot(x, w3_ref[...].astype(BF16))
    o_ref[...] = (_silu(a) * b).astype(o_ref.dtype)


def _ffn_up(xb, w1, w3, layer):
    m, k = xb.shape
    n = w1.shape[-1]
    tm = _tile(m, 2080, 16)
    tn = _tile(n, 256, 128)
    return pl.pallas_call(
        _glu_kernel,
        grid=(m // tm, n // tn),
        in_specs=[_resident_rows(tm, k),
                  _w_spec(w1, layer, k, tn),
                  _w_spec(w3, layer, k, tn)],
        out_specs=pl.BlockSpec((tm, tn), lambda i, j: (i, j)),
        out_shape=jax.ShapeDtypeStruct((m, n), BF16),
        compiler_params=_params("parallel", "arbitrary"),
        name="ffn_up",
    )(xb, w1, w3)


def _mm_kernel(x_ref, w_ref, o_ref):
    o_ref[...] = _dot(x_ref[...], w_ref[...].astype(BF16)).astype(o_ref.dtype)


def _mm_nt_bias_kernel(x_ref, wt_ref, b_ref, o_ref):
    o_ref[...] = (_dot_nt(x_ref[...], wt_ref[...].astype(BF16)) + b_ref[...]).astype(o_ref.dtype)


def _cast_kernel(w_ref, o_ref):
    o_ref[...] = w_ref[...].astype(o_ref.dtype)


def _cast_rows(w, layer, row0, rows):
    _, _, k = w.shape
    tr = _tile(rows, 512, 16)
    assert row0 % 8 == 0
    return pl.pallas_call(
        _cast_kernel,
        grid=(rows // tr,),
        in_specs=[pl.BlockSpec((None, pl.Element(tr), pl.Element(k)),
                               lambda r: (layer, pl.multiple_of(row0 + r * tr, 8), 0))],
        out_specs=pl.BlockSpec((tr, k), lambda r: (r, 0)),
        out_shape=jax.ShapeDtypeStruct((rows, k), BF16),
        compiler_params=_params("parallel"),
        name="cast_rows",
    )(w)


def _mm_res_kernel(x_ref, w_ref, *refs, alpha, yscale):
    *r_refs, o_ref = refs
    o_ref[...] = _residual(_stream_tile(r_refs), _dot(x_ref[...], w_ref[...].astype(BF16)), alpha, yscale)


def _matmul(xb, w, layer, *, tm_target, tn_target, name, residual=None, alpha=None, yscale=None):
    m, k = xb.shape
    n = w.shape[2]
    tm = _tile(m, tm_target, 16)
    tn = _tile(n, tn_target, 128)
    tile = pl.BlockSpec((tm, tn), lambda i, j: (i, j))
    in_specs = [_resident_rows(tm, k), _w_spec(w, layer, k, tn)]
    args = [xb, w]
    kern = _mm_kernel
    if residual is not None:
        r_specs, r_args = _stream_specs(residual, tm, tn)
        in_specs += r_specs
        args += r_args
        kern = functools.partial(_mm_res_kernel, alpha=alpha, yscale=yscale)
    return pl.pallas_call(
        kern,
        grid=(m // tm, n // tn),
        in_specs=in_specs,
        out_specs=tile,
        out_shape=jax.ShapeDtypeStruct((m, n), F32),
        compiler_params=_params("parallel", "arbitrary"),
        name=name,
    )(*args)


def _in_proj(xb, w_t, bias, layer, n):
    m, k = xb.shape
    tm = _tile(m, 1664, 16)
    tn = _tile(n, 512, 128)
    return pl.pallas_call(
        _mm_nt_bias_kernel,
        grid=(m // tm, n // tn),
        in_specs=[_resident_rows(tm, k),
                  pl.BlockSpec((None, tn, k), lambda i, j: (layer, j, 0)),
                  pl.BlockSpec((1, tn), lambda i, j: (0, j))],
        out_specs=pl.BlockSpec((tm, tn), lambda i, j: (i, j)),
        out_shape=jax.ShapeDtypeStruct((m, n), F32),
        compiler_params=_params("parallel", "arbitrary"),
        name="in_proj",
    )(xb, w_t, bias)


def _ln_kernel(*refs, alpha, yscale, n_src, n_alias, outputs):
    srcs, (g_ref, b_ref), outs = refs[:n_src], refs[n_src:n_src + 2], list(refs[n_src + 2 + n_alias:])
    t = srcs[0][...] if n_src == 1 else _residual(srcs[0][...], srcs[1][...], alpha, yscale)
    mu, rstd = _row_stats(t)
    o = _apply_norm(t, mu, rstd, g_ref[...], b_ref[...])
    for kind in outputs:
        if kind == "f32":
            outs.pop(0)[...] = o
        elif kind == "bf16":
            outs.pop(0)[...] = o.astype(BF16)
        else:
            for stat in (mu, rstd):
                ref = outs.pop(0)
                ref[...] = jnp.broadcast_to(stat, ref.shape)


def _res_ln(x, y, g, b, *, alpha=None, yscale=None, rows=None, x_row0=0, y_row0=0, out_row0=0, out_rows=None,
            outputs=("f32", "bf16"), into=None):
    d = x.shape[1]
    rows = x.shape[0] if rows is None else rows
    out_rows = rows if out_rows is None else out_rows
    tm = _tile(rows, 416, 16)
    assert x_row0 % tm == 0 and y_row0 % tm == 0 and out_row0 % tm == 0

    def at(row0, width=d):
        return pl.BlockSpec((tm, width), lambda i: (row0 // tm + i, 0))

    vec = pl.BlockSpec((1, d), lambda i: (0, 0))
    out_shape, out_specs = [], []
    for kind in outputs:
        if kind == "stats":
            out_shape += [jax.ShapeDtypeStruct((out_rows, LANES), F32)] * 2
            out_specs += [at(out_row0, LANES)] * 2
        else:
            out_shape.append(jax.ShapeDtypeStruct((out_rows, d), F32 if kind == "f32" else BF16))
            out_specs.append(at(out_row0))
    srcs = [(x, x_row0)] + ([] if y is None else [(y, y_row0)])
    in_specs = [at(r0) for _, r0 in srcs] + [vec, vec]
    args = [a for a, _ in srcs] + [g.reshape(1, d), b.reshape(1, d)]
    aliases = {}
    if into is not None:
        in_specs += [pl.BlockSpec(memory_space=pl.ANY)] * len(into)
        aliases = {len(args) + n: n for n in range(len(into))}
        args += list(into)
    return pl.pallas_call(
        functools.partial(_ln_kernel, alpha=alpha, yscale=yscale, n_src=len(srcs), n_alias=len(aliases),
                          outputs=outputs),
        grid=(rows // tm,),
        in_specs=in_specs,
        out_specs=out_specs,
        out_shape=out_shape,
        input_output_aliases=aliases,
        compiler_params=_params("parallel"),
        name="res_ln",
    )(*args)


def _log_sigmoid(x):
    return jnp.minimum(x, 0.0) - jnp.log1p(jnp.exp(-jnp.abs(x)))


def _logf_kernel(x_ref, wt_ref, bl_ref, wa_ref, ba_ref, o_ref):
    a_lr = _dot_nt(x_ref[...], wt_ref[...].astype(BF16)) + bl_ref[...]
    t = _dot(a_lr.astype(BF16), wa_ref[...]) + ba_ref[...]
    o_ref[...] = _log_sigmoid(t) / GLA_TAU


def _gla_logf(xb, w_in_t, b_in, layer, col0, wa2_pad, ba):
    m, k = xb.shape
    lane, dk = wa2_pad.shape
    assert col0 % lane == 0
    tm = _tile(m, 1040, 16)
    return pl.pallas_call(
        _logf_kernel,
        grid=(m // tm,),
        in_specs=[pl.BlockSpec((tm, k), lambda i: (i, 0)),
                  pl.BlockSpec((None, lane, k), lambda i: (layer, col0 // lane, 0)),
                  pl.BlockSpec((1, lane), lambda i: (0, col0 // lane)),
                  pl.BlockSpec((lane, dk), lambda i: (0, 0)),
                  pl.BlockSpec((1, dk), lambda i: (0, 0))],
        out_specs=pl.BlockSpec((tm, dk), lambda i: (i, 0)),
        out_shape=jax.ShapeDtypeStruct((m, dk), F32),
        compiler_params=_params("parallel"),
        name="gla_logf",
    )(xb, w_in_t, b_in, wa2_pad, ba)


def _upgate_kernel(x_ref, ya_ref, yb_ref, yc_ref, wga_ref, wgb_ref, wgc_ref, bga_ref, bgb_ref, bgc_ref,
                   wa_ref, wb_ref, wc_ref, o_ref):
    x = x_ref[...]
    acc = jax.nn.sigmoid(_dot_nt(x, wga_ref[...]) + bga_ref[...]) * _dot(ya_ref[...], wa_ref[...].astype(BF16))
    acc = acc + jax.nn.sigmoid(_dot_nt(x, wgb_ref[...]) + bgb_ref[...]) * _dot(yb_ref[...], wb_ref[...].astype(BF16))
    acc = acc + jax.nn.sigmoid(_dot_nt(x, wgc_ref[...]) + bgc_ref[...]) * _dot(yc_ref[...], wc_ref[...].astype(BF16))
    o_ref[...] = acc.astype(o_ref.dtype)


def _up_gate(xb, ya, yb, yc, w_g, b_g, w_up_a, w_up_b, w_up_c, layer):
    m, k = xb.shape
    d = w_up_a.shape[-1]
    tm = _tile(m, 1040, 16)
    tn = _tile(d, 256, 128)
    nb = d // tn

    def rows(width):
        return _resident_rows(tm, width)

    def gate_cols(branch):
        return pl.BlockSpec((tn, k), lambda i, j: (branch * nb + j, 0))

    def gate_bias(branch):
        return pl.BlockSpec((1, tn), lambda i, j: (0, branch * nb + j))

    def up_cols(width):
        return pl.BlockSpec((None, width, tn), lambda i, j: (layer, 0, j))

    return pl.pallas_call(
        _upgate_kernel,
        grid=(m // tm, nb),
        in_specs=[rows(k), rows(ya.shape[1]), rows(yb.shape[1]), rows(yc.shape[1]),
                  gate_cols(0), gate_cols(1), gate_cols(2),
                  gate_bias(0), gate_bias(1), gate_bias(2),
                  up_cols(ya.shape[1]), up_cols(yb.shape[1]), up_cols(yc.shape[1])],
        out_specs=pl.BlockSpec((tm, tn), lambda i, j: (i, j)),
        out_shape=jax.ShapeDtypeStruct((m, d), BF16),
        compiler_params=_params("parallel", "arbitrary"),
        name="up_gate",
    )(xb, ya, yb, yc, w_g, w_g, w_g, b_g, b_g, b_g, w_up_a, w_up_b, w_up_c)


def _pe_kernel(x_ref, p_ref, wg_ref, wp_ref, *refs, alpha):
    *r_refs, o_ref = refs
    y = jax.nn.sigmoid(_dot(x_ref[...], wg_ref[...].astype(BF16))) * _dot(p_ref[...], wp_ref[...].astype(BF16))
    o_ref[...] = _residual(_stream_tile(r_refs), y, alpha, None)


def _pe_embed(xb, pb, w_gate, w_p, layer, residual, alpha):
    m, k = xb.shape
    d = w_gate.shape[-1]
    kp = pb.shape[1]
    tm = _tile(m, 1664, 16)
    tn = _tile(d, 512, 128)
    r_specs, r_args = _stream_specs(residual, tm, tn)
    return pl.pallas_call(
        functools.partial(_pe_kernel, alpha=alpha),
        grid=(m // tm, d // tn),
        in_specs=[_resident_rows(tm, k),
                  _resident_rows(tm, kp),
                  _w_spec(w_gate, layer, k, tn),
                  _w_spec(w_p, layer, kp, tn),
                  *r_specs],
        out_specs=pl.BlockSpec((tm, tn), lambda i, j: (i, j)),
        out_shape=jax.ShapeDtypeStruct((m, d), F32),
        compiler_params=_params("parallel", "arbitrary"),
        name="pe_embed",
    )(xb, pb, w_gate, w_p, *r_args)


def _pool_prompt_kernel(a_ref, pw_ref, ps_ref, o_ref, tail_ref, ext_ref, *, tt, gd):
    t = pl.program_id(1)

    @pl.when(t == pl.num_programs(1) - 1)
    def _():
        tail_ref[...] = a_ref[tt - POOL_HALO:tt, :]

    @pl.when(t == 0)
    def _():
        ext_ref[0:POOL_HALO, :] = jnp.zeros((POOL_HALO, ext_ref.shape[1]), F32)

    a = a_ref[...]
    ext_ref[POOL_HALO:POOL_HALO + tt, :] = a
    pos = t * tt + lax.broadcasted_iota(jnp.int32, (tt, 1), 0)
    for gi, w in enumerate(POOL_WINDOWS):
        sl = slice(gi * gd, (gi + 1) * gd)
        acc = a[:, sl]
        for s in range(1, w):
            acc = acc + ext_ref[POOL_HALO - s:POOL_HALO - s + tt, sl]
        cnt = jnp.minimum(pos + 1, w).astype(F32)
        d = acc / cnt - a[:, sl]
        y = _dot(d.astype(BF16), pw_ref[gi])
        o_ref[:, sl] = (y * ps_ref[:, sl]).astype(o_ref.dtype)
    ext_ref[0:POOL_HALO, :] = ext_ref[tt:tt + POOL_HALO, :]


def _pool_prompt(z, pool_w, pool_scale, *, batch, seq, m, width):
    tt = _tile(seq, 256, 16)
    nt = seq // tt
    gd = width // len(POOL_WINDOWS)
    return pl.pallas_call(
        functools.partial(_pool_prompt_kernel, tt=tt, gd=gd),
        grid=(batch, nt),
        in_specs=[pl.BlockSpec((tt, width), lambda b, t: (b * nt + t, 0)),
                  pl.BlockSpec(pool_w.shape, lambda b, t: (0, 0, 0)),
                  pl.BlockSpec((1, width), lambda b, t: (0, 0))],
        out_specs=[pl.BlockSpec((tt, width), lambda b, t: (b * nt + t, 0)),
                   pl.BlockSpec((None, POOL_HALO, width), lambda b, t: (b, 0, 0))],
        out_shape=[jax.ShapeDtypeStruct((m, width), BF16),
                   jax.ShapeDtypeStruct((batch, POOL_HALO, width), F32)],
        scratch_shapes=[pltpu.VMEM((tt + POOL_HALO, width), F32)],
        compiler_params=_params("parallel", "arbitrary"),
        name="pool_prompt",
    )(z, pool_w, pool_scale)


def _pool_sample_kernel(a_ref, buf_ref, pw_ref, ps_ref, ya_in_ref, o_ref, nbuf_ref, *, width, gd):
    del ya_in_ref
    a = a_ref[...]
    for gi, w in enumerate(POOL_WINDOWS):
        sl = slice(gi * gd, (gi + 1) * gd)
        acc = a[:, sl]
        for s in range(1, w):
            r = POOL_BUF - s
            acc = acc + buf_ref[:, r * width + gi * gd:r * width + (gi + 1) * gd]
        d = acc / float(min(PAST_LEN + 1, w)) - a[:, sl]
        y = _dot(d.astype(BF16), pw_ref[gi])
        o_ref[:, sl] = (y * ps_ref[:, sl]).astype(o_ref.dtype)
    nbuf_ref[:, 0:(POOL_BUF - 1) * width] = buf_ref[:, width:POOL_BUF * width]
    nbuf_ref[:, (POOL_BUF - 1) * width:POOL_BUF * width] = a


def _pool_sample(z, buf_flat, layer, pool_w, pool_scale, ya, *, row0, nsamp, width):
    ns = SAMPLE_BLOCK
    gd = width // len(POOL_WINDOWS)
    rb = row0 // ns
    ya, nbuf = pl.pallas_call(
        functools.partial(_pool_sample_kernel, width=width, gd=gd),
        grid=(nsamp // ns,),
        in_specs=[pl.BlockSpec((ns, width), lambda i: (rb + i, 0)),
                  pl.BlockSpec((None, ns, POOL_BUF * width), lambda i: (layer, i, 0)),
                  pl.BlockSpec(pool_w.shape, lambda i: (0, 0, 0)),
                  pl.BlockSpec((1, width), lambda i: (0, 0)),
                  pl.BlockSpec(memory_space=pl.ANY)],
        out_specs=[pl.BlockSpec((ns, width), lambda i: (rb + i, 0)),
                   pl.BlockSpec((ns, POOL_BUF * width), lambda i: (i, 0))],
        out_shape=[jax.ShapeDtypeStruct(ya.shape, ya.dtype),
                   jax.ShapeDtypeStruct(buf_flat.shape[1:], F32)],
        input_output_aliases={4: 0},
        compiler_params=_params("parallel"),
        name="pool_sample",
    )(z, buf_flat, pool_w, pool_scale, ya)
    return ya, nbuf


def _sgu_prompt_kernel(u_ref, v_ref, lng_ref, lnb_ref, ws_ref, biast_ref, o_ref, *, nchunk, gd):
    c = SGU_CHUNK
    tril = (lax.broadcasted_iota(jnp.int32, (c, c), 0) >= lax.broadcasted_iota(jnp.int32, (c, c), 1)).astype(F32)
    w_m = [(ws_ref[g] * tril).astype(BF16) for g in range(SGU_GROUPS)]
    for ci in range(nchunk):
        rows = slice(ci * c, (ci + 1) * c)
        vn = _row_layer_norm(v_ref[rows, :], lng_ref[...], lnb_ref[...]).astype(BF16)
        for g in range(SGU_GROUPS):
            sl = slice(g * gd, (g + 1) * gd)
            mixed = _dot(w_m[g], vn[:, sl]) + biast_ref[:, g:g + 1]
            o_ref[rows, sl] = (u_ref[rows, sl] * mixed).astype(o_ref.dtype)


def _sgu_prompt(z, ln_g, ln_b, ws, bias_t, *, batch, seq, m, width, u_blk, v_blk):
    assert seq % SGU_CHUNK == 0
    rows = _tile(seq, 512, SGU_CHUNK)
    nt = seq // rows
    gd = width // SGU_GROUPS
    return pl.pallas_call(
        functools.partial(_sgu_prompt_kernel, nchunk=rows // SGU_CHUNK, gd=gd),
        grid=(batch * nt,),
        in_specs=[pl.BlockSpec((rows, width), lambda i: (i, u_blk)),
                  pl.BlockSpec((rows, width), lambda i: (i, v_blk)),
                  pl.BlockSpec((1, width), lambda i: (0, 0)),
                  pl.BlockSpec((1, width), lambda i: (0, 0)),
                  pl.BlockSpec(ws.shape, lambda i: (0, 0, 0)),
                  pl.BlockSpec(bias_t.shape, lambda i: (0, 0))],
        out_specs=pl.BlockSpec((rows, width), lambda i: (i, 0)),
        out_shape=jax.ShapeDtypeStruct((m, width), BF16),
        compiler_params=_params("parallel"),
        name="sgu_prompt",
    )(z, z, ln_g, ln_b, ws, bias_t)


def _sgu_sample_kernel(u_ref, v_ref, lng_ref, lnb_ref, w0_ref, b0_ref, yb_in_ref, o_ref, vn_ref):
    del yb_in_ref
    vn = _row_layer_norm(v_ref[...], lng_ref[...], lnb_ref[...])
    vn_ref[...] = vn
    mixed = _round_bf16(w0_ref[...]) * _round_bf16(vn) + b0_ref[...]
    o_ref[...] = (u_ref[...] * mixed).astype(o_ref.dtype)


def _sgu_sample(z, ln_g, ln_b, w0, b0, yb, *, row0, nsamp, width, u_blk, v_blk):
    ns = SAMPLE_BLOCK
    rb = row0 // ns
    vec = pl.BlockSpec((1, width), lambda i: (0, 0))
    yb, vn = pl.pallas_call(
        _sgu_sample_kernel,
        grid=(nsamp // ns,),
        in_specs=[pl.BlockSpec((ns, width), lambda i: (rb + i, u_blk)),
                  pl.BlockSpec((ns, width), lambda i: (rb + i, v_blk)),
                  vec, vec, vec, vec,
                  pl.BlockSpec(memory_space=pl.ANY)],
        out_specs=[pl.BlockSpec((ns, width), lambda i: (rb + i, 0)),
                   pl.BlockSpec((ns, width), lambda i: (i, 0))],
        out_shape=[jax.ShapeDtypeStruct(yb.shape, yb.dtype),
                   jax.ShapeDtypeStruct((nsamp, width), F32)],
        input_output_aliases={6: 0},
        compiler_params=_params("parallel"),
        name="sgu_sample",
    )(z, z, ln_g, ln_b, w0, b0, yb)
    return yb, vn


def _cumsum_rows(x):
    rows = x.shape[0]
    row = lax.broadcasted_iota(jnp.int32, x.shape, 0)
    s = 1
    while s < rows:
        x = x + jnp.where(row >= s, pltpu.roll(x, s, 0), 0.0)
        s *= 2
    return x


def _gla_out(o, norm_g, r):
    o = o * lax.rsqrt(jnp.mean(o * o, -1, keepdims=True) + EPS) * norm_g
    return o * _silu(r)


def _gla_prompt_kernel(q_ref, k_ref, v01_ref, v23_ref, g_ref, r01_ref, r23_ref, ng_ref, o_ref, s_ref, st_ref, *,
                       nchunk, scale, hk, hv):
    t = pl.program_id(1)
    c = GLA_CHUNK

    @pl.when(t == 0)
    def _():
        st_ref[...] = jnp.zeros(st_ref.shape, F32)

    causal = (lax.broadcasted_iota(jnp.int32, (c, c), 0) >= lax.broadcasted_iota(jnp.int32, (c, c), 1)).astype(F32)
    v_refs = (v01_ref, v23_ref)
    r_refs = (r01_ref, r23_ref)
    for ci in range(nchunk):
        rows = slice(ci * c, (ci + 1) * c)
        for h in range(GLA_HEADS):
            kcols = slice(h * hk, (h + 1) * hk)
            vcols = slice((h % 2) * hv, (h % 2 + 1) * hv)
            ocols = slice(h * hv, (h + 1) * hv)
            b = _cumsum_rows(g_ref[rows, kcols])
            b_end = b[c - 1:c, :]
            k = k_ref[rows, kcols]
            q_in = ((q_ref[rows, kcols] * scale) * jnp.exp(b)).astype(BF16)
            k_in = (k * jnp.exp(-b)).astype(BF16)
            k_dec = (k * jnp.exp(b_end - b)).astype(BF16)
            vb = v_refs[h // 2][rows, vcols].astype(BF16)
            att = _dot_nt(q_in, k_in) * causal
            st = st_ref[h]
            o = _dot(att.astype(BF16), vb) + _dot_nt(q_in, st.astype(BF16))
            st_ref[h] = st * jnp.exp(b_end) + _dot_tn(vb, k_dec)
            o_ref[rows, ocols] = _gla_out(o, ng_ref[:, ocols], r_refs[h // 2][rows, vcols]).astype(o_ref.dtype)

    @pl.when(t == pl.num_programs(1) - 1)
    def _():
        for h in range(GLA_HEADS):
            s_ref[0, h] = st_ref[h].T


def _gla_prompt(z, logf, norm_g, *, batch, seq, m, hk, hv, q_off):
    assert seq % GLA_CHUNK == 0 and GLA_HEADS == 4
    rows = _tile(seq, 256, GLA_CHUNK)
    nt = seq // rows
    heads = GLA_HEADS
    dk, dv2 = heads * hk, 2 * hv
    v_off = q_off + 2 * dk
    assert q_off % dk == 0 and v_off % dv2 == 0

    def cols(width, off):
        return pl.BlockSpec((rows, width), lambda n, t: (n * nt + t, off // width))

    yc, s = pl.pallas_call(
        functools.partial(_gla_prompt_kernel, nchunk=rows // GLA_CHUNK, scale=hk ** -0.5, hk=hk, hv=hv),
        grid=(batch, nt),
        in_specs=[cols(dk, q_off), cols(dk, q_off + dk),
                  cols(dv2, v_off), cols(dv2, v_off + dv2),
                  cols(dk, 0),
                  cols(dv2, v_off + 2 * dv2), cols(dv2, v_off + 3 * dv2),
                  pl.BlockSpec((1, heads * hv), lambda n, t: (0, 0))],
        out_specs=[pl.BlockSpec((rows, heads * hv), lambda n, t: (n * nt + t, 0)),
                   pl.BlockSpec((1, heads, hk, hv), lambda n, t: (n, 0, 0, 0))],
        out_shape=[jax.ShapeDtypeStruct((m, heads * hv), BF16),
                   jax.ShapeDtypeStruct((batch, heads, hk, hv), F32)],
        scratch_shapes=[pltpu.VMEM((heads, hv, hk), F32)],
        compiler_params=_params("parallel", "arbitrary"),
        name="gla_prompt",
    )(z, z, z, z, logf, z, z, norm_g)
    return yc, s


def _gla_sample_kernel(*refs, scale):
    q_ref, k_ref, v_ref, g_ref, r_ref, ng_ref, s_ref, yc_in_ref = refs[:8]
    o_ref, so_ref = refs[-2:]
    del yc_in_ref
    ns = SAMPLE_BLOCK
    g = g_ref[...]
    e = jnp.exp(g)
    k = k_ref[...]
    q_in = _round_bf16((q_ref[...] * scale) * e)
    k_in = _round_bf16(k * jnp.exp(-g))
    k_dec = _round_bf16(k * jnp.exp(g - g))
    vb = _round_bf16(v_ref[...])
    att = jnp.sum(q_in * k_in, -1, keepdims=True)
    hk = g.shape[1]
    cols = jnp.concatenate([e, k_dec, q_in, jnp.zeros((128 - 3 * ns, hk), F32)], axis=0).T
    outs = []
    for j in range(ns):
        s = s_ref[0, j, 0]
        so_ref[0, j, 0] = s * cols[:, j:j + 1] + cols[:, ns + j:ns + j + 1] * vb[j:j + 1, :]
        outs.append(jnp.sum(cols[:, 2 * ns + j:2 * ns + j + 1] * _round_bf16(s), axis=0, keepdims=True))
    o = jnp.concatenate(outs, axis=0) + _round_bf16(att) * vb
    o_ref[...] = _gla_out(o, ng_ref[...], r_ref[...]).astype(o_ref.dtype)


def _gla_sample(z, logf, norm_g, state, layer, yc, prev_out, *, row0, nsamp, hk, hv, q_blk, k_blk, v_blk, r_blk):
    ns = SAMPLE_BLOCK
    rb = row0 // ns
    heads = GLA_HEADS
    st_spec = pl.BlockSpec((1, ns, 1, hk, hv), lambda i, h: (layer, i, h, 0, 0))
    in_specs = [pl.BlockSpec((ns, hk), lambda i, h: (rb + i, q_blk + h)),
                pl.BlockSpec((ns, hk), lambda i, h: (rb + i, k_blk + h)),
                pl.BlockSpec((ns, hv), lambda i, h: (rb + i, v_blk + h)),
                pl.BlockSpec((ns, hk), lambda i, h: (rb + i, h)),
                pl.BlockSpec((ns, hv), lambda i, h: (rb + i, r_blk + h)),
                pl.BlockSpec((1, hv), lambda i, h: (0, h)),
                st_spec,
                pl.BlockSpec(memory_space=pl.ANY)]
    args = [z, z, z, logf, z, norm_g, state, yc]
    aliases = {7: 0}
    if prev_out is not None:
        in_specs.append(pl.BlockSpec(memory_space=pl.ANY))
        args.append(prev_out)
        aliases[8] = 1
    yc, s_out = pl.pallas_call(
        functools.partial(_gla_sample_kernel, scale=hk ** -0.5),
        grid=(nsamp // ns, heads),
        in_specs=in_specs,
        out_specs=[pl.BlockSpec((ns, hv), lambda i, h: (rb + i, h)), st_spec],
        out_shape=[jax.ShapeDtypeStruct(yc.shape, yc.dtype),
                   jax.ShapeDtypeStruct(state.shape, F32)],
        input_output_aliases=aliases,
        compiler_params=_params("parallel", "arbitrary"),
        name="gla_sample",
    )(*args)
    return yc, s_out


def _norm_stream(t, g, b):
    g, b = g.reshape(1, -1), b.reshape(1, -1)
    xb, mu, rstd = _res_ln(t, None, g, b, outputs=("bf16", "stats"))
    return _Normed(t, mu, rstd, g, b), xb


def _ffn(x, xb, w1, w3, w2, layer, g, b, alpha):
    h = _ffn_up(xb, w1, w3, layer)
    down = functools.partial(_matmul, h, w2, layer, tm_target=1040, tn_target=256, name="ffn_down")
    if not isinstance(x, tuple) or isinstance(x, _Normed):
        return _norm_stream(down(residual=x, alpha=alpha, yscale=0.5), g, b)
    y = down()
    ln = functools.partial(_res_ln, g=g, b=b, alpha=alpha, yscale=0.5)
    x_p, x_s = x
    mp, m = x_p.shape[0], y.shape[0]
    outs = ln(x_p, y, out_rows=m)
    return ln(x_s, y, y_row0=mp, out_row0=mp, out_rows=m, into=outs)


def kernel(x_prompt, x_sample, state_pool, state_gla, p_prompt, p_sample, ffa_w1, ffa_w3, ffa_w2, ffb_w1, ffb_w3, ffb_w2, ln_g, ln_b, w_in, b_in, pool_w, pool_scale, sgu_ln_g, sgu_ln_b, sgu_ws, sgu_bias, gla_wa2, gla_ba, gla_norm_g, w_up_a, w_up_b, w_up_c, w_o, pe_w, pe_gate_w):
    depth = ffa_w1.shape[0]
    batch, seq, d = x_prompt.shape
    nsamp, dec_seq, _ = x_sample.shape
    assert dec_seq == 1 and nsamp % SAMPLE_BLOCK == 0
    mp = batch * seq
    m = mp + nsamp
    assert mp % SAMPLE_BLOCK == 0
    alpha = (2.0 * depth) ** 0.25

    pool_wd = pool_w.shape[1] * pool_w.shape[2]
    sgu_wd = sgu_ln_g.shape[1]
    dk = gla_wa2.shape[2]
    dv = gla_norm_g.shape[1]
    hk, hv = dk // GLA_HEADS, dv // GLA_HEADS
    n_main = pool_wd + 2 * sgu_wd + 2 * dk + 2 * dv
    n_gate = n_main + GLA_RANK
    assert w_in.shape[2] == n_gate + 3 * d
    assert pool_wd == sgu_wd
    u_blk, v_blk = pool_wd // sgu_wd, pool_wd // sgu_wd + 1
    q_off = pool_wd + 2 * sgu_wd
    q_blk, k_blk = q_off // hk, (q_off + dk) // hk
    v2_blk, r_blk = (q_off + 2 * dk) // hv, (q_off + 2 * dk + dv) // hv

    x_p2, x_s2 = x_prompt.reshape(mp, d), x_sample.reshape(nsamp, d)
    x = (x_p2, x_s2)
    xb = jnp.concatenate([x_p2, x_s2], axis=0).astype(BF16)
    w_in_t = jnp.swapaxes(w_in, 1, 2)

    pool_p, gla_p, pool_s, sgu_s = [], [], [], []
    gla_s = None
    state_pool_flat = state_pool.reshape(depth, nsamp, POOL_BUF * pool_wd)
    for i in range(depth):
        x, xb = _ffn(x, xb, ffa_w1, ffa_w3, ffa_w2, i, ln_g[i, 0], ln_b[i, 0], alpha)

        b_in_i = b_in[i].reshape(1, -1)
        z = _in_proj(xb, w_in_t, b_in_i, i, n_main)
        logf = _gla_logf(xb, w_in_t, b_in_i, i, n_main,
                         jnp.pad(gla_wa2[i].astype(BF16), ((0, 128 - GLA_RANK), (0, 0))),
                         gla_ba[i].reshape(1, dk))

        pw = pool_w[i].astype(BF16)
        ps = pool_scale[i].reshape(1, pool_wd)
        ya, tail = _pool_prompt(z, pw, ps, batch=batch, seq=seq, m=m, width=pool_wd)
        ya, nbuf = _pool_sample(z, state_pool_flat, i, pw, ps, ya, row0=mp, nsamp=nsamp, width=pool_wd)
        pool_p.append(tail[:, POOL_HALO - POOL_BUF:])
        pool_s.append(nbuf.reshape(nsamp, POOL_BUF, pool_wd))

        lng = sgu_ln_g[i].reshape(1, sgu_wd)
        lnb = sgu_ln_b[i].reshape(1, sgu_wd)
        gd = sgu_wd // SGU_GROUPS
        yb = _sgu_prompt(z, lng, lnb, sgu_ws[i], sgu_bias[i].T, batch=batch, seq=seq, m=m, width=sgu_wd,
                         u_blk=u_blk, v_blk=v_blk)
        yb, vn = _sgu_sample(z, lng, lnb,
                             jnp.repeat(sgu_ws[i, :, 0, 0], gd).reshape(1, sgu_wd),
                             jnp.repeat(sgu_bias[i, :, 0], gd).reshape(1, sgu_wd),
                             yb, row0=mp, nsamp=nsamp, width=sgu_wd, u_blk=u_blk, v_blk=v_blk)
        sgu_s.append(vn.reshape(nsamp, 1, sgu_wd))

        ng = gla_norm_g[i].reshape(1, dv)
        blks = dict(hk=hk, hv=hv, q_blk=q_blk, k_blk=k_blk, v_blk=v2_blk, r_blk=r_blk)
        yc, s_p = _gla_prompt(z, logf, ng, batch=batch, seq=seq, m=m, hk=hk, hv=hv, q_off=q_off)
        yc, gla_s = _gla_sample(z, logf, ng, state_gla, i, yc, gla_s, row0=mp, nsamp=nsamp, **blks)
        gla_p.append(s_p)

        w_g = _cast_rows(w_in_t, i, n_gate, 3 * d)
        h = _up_gate(xb, ya, yb, yc, w_g, b_in_i[:, n_gate:], w_up_a, w_up_b, w_up_c, i)
        t = _matmul(h, w_o, i, tm_target=1664, tn_target=512, name="out_proj", residual=x, alpha=alpha)
        x, xb = _norm_stream(t, ln_g[i, 1], ln_b[i, 1])

        x, xb = _ffn(x, xb, ffb_w1, ffb_w3, ffb_w2, i, ln_g[i, 2], ln_b[i, 2], alpha)

        pb = jnp.concatenate([p_prompt[i].reshape(mp, -1), p_sample[i].reshape(nsamp, -1)], axis=0).astype(BF16)
        t = _pe_embed(xb, pb, pe_gate_w, pe_w, i, x, alpha)
        if i + 1 < depth:
            x, xb = _norm_stream(t, ln_g[i, 3], ln_b[i, 3])
        else:
            ln = functools.partial(_res_ln, t, None, ln_g[i, 3], ln_b[i, 3], outputs=("f32",))
            (x_p,), (x_s,) = ln(rows=mp), ln(rows=nsamp, x_row0=mp)

    return (x_p.reshape(batch, seq, d), x_s.reshape(nsamp, 1, d),
            jnp.stack(pool_p), jnp.stack(gla_p).astype(state_gla.dtype),
            jnp.stack(pool_s), gla_s.astype(state_gla.dtype), jnp.stack(sgu_s))
```

```python
import functools
from typing import NamedTuple

import jax
import jax.numpy as jnp
from jax import lax
from jax.experimental import pallas as pl
from jax.experimental.pallas import tpu as pltpu

F32 = jnp.float32
BF16 = jnp.bfloat16

POOL_WINDOWS = (2, 4, 8, 16)
POOL_BUF = max(POOL_WINDOWS) - 1
POOL_HALO = 16
SGU_GROUPS = 8
SGU_CHUNK = 128
GLA_HEADS = 4
GLA_RANK = 16
GLA_TAU = 16.0
GLA_CHUNK = 64
EPS = 1e-5
PAST_LEN = 16384
SAMPLE_BLOCK = 8

LANES = 128
V7X_VMEM_LIMIT_BYTES = 60 * 1024 * 1024


def _tile(dim, target, align):
    best = None
    t = align
    while t <= min(dim, target):
        if dim % t == 0:
            best = t
        t += align
    return dim if best is None else best


def _params(*sem):
    return pltpu.CompilerParams(dimension_semantics=sem, vmem_limit_bytes=V7X_VMEM_LIMIT_BYTES)


def _silu(x):
    return x * jax.nn.sigmoid(x)


def _dot(a, b):
    return jnp.dot(a, b, preferred_element_type=F32)


def _dot_nt(a, b):
    return lax.dot_general(a, b, (((1,), (1,)), ((), ())), preferred_element_type=F32)


def _dot_tn(a, b):
    return lax.dot_general(a, b, (((0,), (0,)), ((), ())), preferred_element_type=F32)


def _round_bf16(x):
    return x.astype(BF16).astype(F32)


def _row_stats(v):
    mu = jnp.mean(v, -1, keepdims=True)
    d = v - mu
    return mu, lax.rsqrt(jnp.mean(d * d, -1, keepdims=True) + EPS)


def _apply_norm(v, mu, rstd, g, b):
    return (v - mu) * rstd * g + b


def _row_layer_norm(v, g, b):
    mu, rstd = _row_stats(v)
    return _apply_norm(v, mu, rstd, g, b)


def _residual(x, y, alpha, yscale):
    return alpha * x + (y if yscale is None else yscale * y)


class _Normed(NamedTuple):
    t: jax.Array
    mu: jax.Array
    rstd: jax.Array
    g: jax.Array
    b: jax.Array


def _stream_specs(x, tm, tn):
    tile = pl.BlockSpec((tm, tn), lambda i, j: (i, j))
    if not isinstance(x, _Normed):
        return [tile], [x]
    stat = pl.BlockSpec((tm, LANES), lambda i, j: (i, 0))
    vec = pl.BlockSpec((1, tn), lambda i, j: (0, j))
    return [tile, stat, stat, vec, vec], list(x)


def _stream_tile(refs):
    if len(refs) == 1:
        return refs[0][...]
    t_ref, mu_ref, rstd_ref, g_ref, b_ref = refs
    return _apply_norm(t_ref[...], mu_ref[:, 0:1], rstd_ref[:, 0:1], g_ref[...], b_ref[...])


def _w_spec(w, layer, k, tn):
    assert w.ndim == 3 and w.shape[1] == k
    return pl.BlockSpec((None, k, tn), lambda i, j: (layer, 0, j))


def _resident_rows(tm, k):
    return pl.BlockSpec((tm, k), lambda i, j: (i, 0), pipeline_mode=pl.Buffered(1))


def _glu_kernel(x_ref, w1_ref, w3_ref, o_ref):
    x = x_ref[...]
    a = _dot(x, w1_ref[...].astype(BF16))
    b = _dot(x, w3_ref[...].astype(BF16))
    o_ref[...] = (_silu(a) * b).astype(o_ref.dtype)


def _ffn_up(xb, w1, w3, layer):
    m, k = xb.shape
    n = w1.shape[-1]
    tm = _tile(m, 2080, 16)
    tn = _tile(n, 256, 128)
    return pl.pallas_call(
        _glu_kernel,
        grid=(m // tm, n // tn),
        in_specs=[_resident_rows(tm, k),
                  _w_spec(w1, layer, k, tn),
                  _w_spec(w3, layer, k, tn)],
        out_specs=pl.BlockSpec((tm, tn), lambda i, j: (i, j)),
        out_shape=jax.ShapeDtypeStruct((m, n), BF16),
        compiler_params=_params("parallel", "arbitrary"),
        name="ffn_up",
    )(xb, w1, w3)


def _mm_kernel(x_ref, w_ref, o_ref):
    o_ref[...] = _dot(x_ref[...], w_ref[...].astype(BF16)).astype(o_ref.dtype)


def _mm_nt_bias_kernel(x_ref, wt_ref, b_ref, o_ref):
    o_ref[...] = (_dot_nt(x_ref[...], wt_ref[...].astype(BF16)) + b_ref[...]).astype(o_ref.dtype)


def _cast_kernel(w_ref, o_ref):
    o_ref[...] = w_ref[...].astype(o_ref.dtype)


def _cast_into_kernel(x_ref, *refs):
    refs[-1][...] = x_ref[...].astype(refs[-1].dtype)


def _stack_cast(parts, dtype):
    d = parts[0].shape[1]
    m = sum(p.shape[0] for p in parts)
    out, row0 = None, 0
    for p in parts:
        rows = p.shape[0]
        tm = _tile(rows, 512, 16)
        assert row0 % tm == 0
        earlier = [] if out is None else [out]
        out = pl.pallas_call(
            _cast_into_kernel,
            grid=(rows // tm,),
            in_specs=[pl.BlockSpec((tm, d), lambda i: (i, 0))] + [pl.BlockSpec(memory_space=pl.ANY)] * len(earlier),
            out_specs=pl.BlockSpec((tm, d), lambda i, blk0=row0 // tm: (blk0 + i, 0)),
            out_shape=jax.ShapeDtypeStruct((m, d), dtype),
            input_output_aliases={1: 0} if earlier else {},
            compiler_params=_params("parallel"),
            name="stack_cast",
        )(p, *earlier)
        row0 += rows
    return out


def _gate_tile(d):
    return _tile(d, 256, 128)


def _cast_gate_rows(w, layer, row0, d, branches):
    _, _, k = w.shape
    tr = _gate_tile(d)
    nb = d // tr
    assert row0 % 8 == 0
    return pl.pallas_call(
        _cast_kernel,
        grid=(branches * nb,),
        in_specs=[pl.BlockSpec((None, pl.Element(tr), pl.Element(k)),
                               lambda r: (layer, pl.multiple_of(row0 + r * tr, 8), 0))],
        out_specs=pl.BlockSpec((tr, k), lambda r: ((r % nb) * branches + r // nb, 0)),
        out_shape=jax.ShapeDtypeStruct((branches * d, k), BF16),
        compiler_params=_params("parallel"),
        name="cast_rows",
    )(w)


def _mm_res_kernel(x_ref, w_ref, *refs, alpha, yscale):
    *r_refs, o_ref = refs
    o_ref[...] = _residual(_stream_tile(r_refs), _dot(x_ref[...], w_ref[...].astype(BF16)), alpha, yscale)


def _matmul(xb, w, layer, *, tm_target, tn_target, name, residual=None, alpha=None, yscale=None):
    m, k = xb.shape
    n = w.shape[2]
    tm = _tile(m, tm_target, 16)
    tn = _tile(n, tn_target, 128)
    tile = pl.BlockSpec((tm, tn), lambda i, j: (i, j))
    in_specs = [_resident_rows(tm, k), _w_spec(w, layer, k, tn)]
    args = [xb, w]
    kern = _mm_kernel
    if residual is not None:
        r_specs, r_args = _stream_specs(residual, tm, tn)
        in_specs += r_specs
        args += r_args
        kern = functools.partial(_mm_res_kernel, alpha=alpha, yscale=yscale)
    return pl.pallas_call(
        kern,
        grid=(m // tm, n // tn),
        in_specs=in_specs,
        out_specs=tile,
        out_shape=jax.ShapeDtypeStruct((m, n), F32),
        compiler_params=_params("parallel", "arbitrary"),
        name=name,
    )(*args)


def _in_proj(xb, w_t, bias, layer, n):
    m, k = xb.shape
    tm = _tile(m, 1664, 16)
    tn = _tile(n, 512, 128)
    return pl.pallas_call(
        _mm_nt_bias_kernel,
        grid=(m // tm, n // tn),
        in_specs=[_resident_rows(tm, k),
                  pl.BlockSpec((None, tn, k), lambda i, j: (layer, j, 0)),
                  pl.BlockSpec((1, tn), lambda i, j: (0, j))],
        out_specs=pl.BlockSpec((tm, tn), lambda i, j: (i, j)),
        out_shape=jax.ShapeDtypeStruct((m, n), F32),
        compiler_params=_params("parallel", "arbitrary"),
        name="in_proj",
    )(xb, w_t, bias)


def _ln_kernel(*refs, alpha, yscale, n_src, n_alias, outputs):
    srcs, (g_ref, b_ref), outs = refs[:n_src], refs[n_src:n_src + 2], list(refs[n_src + 2 + n_alias:])
    t = srcs[0][...] if n_src == 1 else _residual(srcs[0][...], srcs[1][...], alpha, yscale)
    mu, rstd = _row_stats(t)
    o = _apply_norm(t, mu, rstd, g_ref[...], b_ref[...])
    for kind in outputs:
        if kind == "f32":
            outs.pop(0)[...] = o
        elif kind == "bf16":
            outs.pop(0)[...] = o.astype(BF16)
        else:
            for stat in (mu, rstd):
                ref = outs.pop(0)
                ref[...] = jnp.broadcast_to(stat, ref.shape)


def _res_ln(x, y, g, b, *, alpha=None, yscale=None, rows=None, x_row0=0, y_row0=0, out_row0=0, out_rows=None,
            outputs=("f32", "bf16"), into=None):
    d = x.shape[1]
    rows = x.shape[0] if rows is None else rows
    out_rows = rows if out_rows is None else out_rows
    tm = _tile(rows, 416, 16)
    assert x_row0 % tm == 0 and y_row0 % tm == 0 and out_row0 % tm == 0

    def at(row0, width=d):
        return pl.BlockSpec((tm, width), lambda i: (row0 // tm + i, 0))

    vec = pl.BlockSpec((1, d), lambda i: (0, 0))
    out_shape, out_specs = [], []
    for kind in outputs:
        if kind == "stats":
            out_shape += [jax.ShapeDtypeStruct((out_rows, LANES), F32)] * 2
            out_specs += [at(out_row0, LANES)] * 2
        else:
            out_shape.append(jax.ShapeDtypeStruct((out_rows, d), F32 if kind == "f32" else BF16))
            out_specs.append(at(out_row0))
    srcs = [(x, x_row0)] + ([] if y is None else [(y, y_row0)])
    in_specs = [at(r0) for _, r0 in srcs] + [vec, vec]
    args = [a for a, _ in srcs] + [g.reshape(1, d), b.reshape(1, d)]
    aliases = {}
    if into is not None:
        in_specs += [pl.BlockSpec(memory_space=pl.ANY)] * len(into)
        aliases = {len(args) + n: n for n in range(len(into))}
        args += list(into)
    return pl.pallas_call(
        functools.partial(_ln_kernel, alpha=alpha, yscale=yscale, n_src=len(srcs), n_alias=len(aliases),
                          outputs=outputs),
        grid=(rows // tm,),
        in_specs=in_specs,
        out_specs=out_specs,
        out_shape=out_shape,
        input_output_aliases=aliases,
        compiler_params=_params("parallel"),
        name="res_ln",
    )(*args)


def _log_sigmoid(x):
    return jnp.minimum(x, 0.0) - jnp.log1p(jnp.exp(-jnp.abs(x)))


def _logf_kernel(x_ref, wt_ref, bl_ref, wa_ref, ba_ref, o_ref):
    a_lr = _dot_nt(x_ref[...], wt_ref[...].astype(BF16)) + bl_ref[...]
    a_lr = jnp.where(lax.broadcasted_iota(jnp.int32, a_lr.shape, 1) < GLA_RANK, a_lr, 0.0)
    t = _dot(a_lr.astype(BF16), wa_ref[...]) + ba_ref[...]
    o_ref[...] = _log_sigmoid(t) / GLA_TAU


def _gla_logf(xb, w_in_t, b_in, layer, col0, wa2_pad, ba):
    m, k = xb.shape
    lane, dk = wa2_pad.shape
    assert col0 % lane == 0
    tm = _tile(m, 1040, 16)
    return pl.pallas_call(
        _logf_kernel,
        grid=(m // tm,),
        in_specs=[pl.BlockSpec((tm, k), lambda i: (i, 0)),
                  pl.BlockSpec((None, lane, k), lambda i: (layer, col0 // lane, 0)),
                  pl.BlockSpec((1, lane), lambda i: (0, col0 // lane)),
                  pl.BlockSpec((lane, dk), lambda i: (0, 0)),
                  pl.BlockSpec((1, dk), lambda i: (0, 0))],
        out_specs=pl.BlockSpec((tm, dk), lambda i: (i, 0)),
        out_shape=jax.ShapeDtypeStruct((m, dk), F32),
        compiler_params=_params("parallel"),
        name="gla_logf",
    )(xb, w_in_t, b_in, wa2_pad, ba)


def _upgate_kernel(x_ref, ya_ref, yb_ref, yc_ref, wg_ref, bg_ref, wa_ref, wb_ref, wc_ref, o_ref):
    tn = o_ref.shape[1]
    gates = jax.nn.sigmoid(_dot_nt(x_ref[...], wg_ref[...]) + bg_ref[...])
    acc = gates[:, :tn] * _dot(ya_ref[...], wa_ref[...].astype(BF16))
    acc = acc + gates[:, tn:2 * tn] * _dot(yb_ref[...], wb_ref[...].astype(BF16))
    acc = acc + gates[:, 2 * tn:] * _dot(yc_ref[...], wc_ref[...].astype(BF16))
    o_ref[...] = acc.astype(o_ref.dtype)


def _up_gate(xb, ya, yb, yc, w_g, b_g, w_up_a, w_up_b, w_up_c, layer):
    m, k = xb.shape
    d = w_up_a.shape[-1]
    tm = _tile(m, 1040, 16)
    tn = _gate_tile(d)

    def rows(width):
        return _resident_rows(tm, width)

    def up_cols(width):
        return pl.BlockSpec((None, width, tn), lambda i, j: (layer, 0, j))

    return pl.pallas_call(
        _upgate_kernel,
        grid=(m // tm, d // tn),
        in_specs=[rows(k), rows(ya.shape[1]), rows(yb.shape[1]), rows(yc.shape[1]),
                  pl.BlockSpec((3 * tn, k), lambda i, j: (j, 0)),
                  pl.BlockSpec((1, 3 * tn), lambda i, j: (0, j)),
                  up_cols(ya.shape[1]), up_cols(yb.shape[1]), up_cols(yc.shape[1])],
        out_specs=pl.BlockSpec((tm, tn), lambda i, j: (i, j)),
        out_shape=jax.ShapeDtypeStruct((m, d), BF16),
        compiler_params=_params("parallel", "arbitrary"),
        name="up_gate",
    )(xb, ya, yb, yc, w_g, b_g, w_up_a, w_up_b, w_up_c)


def _pe_kernel(x_ref, p_ref, wg_ref, wp_ref, *refs, alpha):
    *r_refs, o_ref = refs
    y = jax.nn.sigmoid(_dot(x_ref[...], wg_ref[...].astype(BF16))) * _dot(p_ref[...], wp_ref[...].astype(BF16))
    o_ref[...] = _residual(_stream_tile(r_refs), y, alpha, None)


def _pe_embed(xb, pb, w_gate, w_p, layer, residual, alpha):
    m, k = xb.shape
    d = w_gate.shape[-1]
    kp = pb.shape[1]
    tm = _tile(m, 1664, 16)
    tn = _tile(d, 512, 128)
    r_specs, r_args = _stream_specs(residual, tm, tn)
    return pl.pallas_call(
        functools.partial(_pe_kernel, alpha=alpha),
        grid=(m // tm, d // tn),
        in_specs=[_resident_rows(tm, k),
                  _resident_rows(tm, kp),
                  _w_spec(w_gate, layer, k, tn),
                  _w_spec(w_p, layer, kp, tn),
                  *r_specs],
        out_specs=pl.BlockSpec((tm, tn), lambda i, j: (i, j)),
        out_shape=jax.ShapeDtypeStruct((m, d), F32),
        compiler_params=_params("parallel", "arbitrary"),
        name="pe_embed",
    )(xb, pb, w_gate, w_p, *r_args)


def _pool_prompt_kernel(a_ref, pw_ref, ps_ref, o_ref, tail_ref, ext_ref, *, tt, gd):
    t = pl.program_id(1)

    @pl.when(t == pl.num_programs(1) - 1)
    def _():
        tail_ref[...] = a_ref[tt - POOL_HALO:tt, :]

    @pl.when(t == 0)
    def _():
        ext_ref[0:POOL_HALO, :] = jnp.zeros((POOL_HALO, ext_ref.shape[1]), F32)

    a = a_ref[...]
    ext_ref[POOL_HALO:POOL_HALO + tt, :] = a
    pos = t * tt + lax.broadcasted_iota(jnp.int32, (tt, 1), 0)
    for gi, w in enumerate(POOL_WINDOWS):
        sl = slice(gi * gd, (gi + 1) * gd)
        acc = a[:, sl]
        for s in range(1, w):
            acc = acc + ext_ref[POOL_HALO - s:POOL_HALO - s + tt, sl]
        cnt = jnp.minimum(pos + 1, w).astype(F32)
        d = acc / cnt - a[:, sl]
        y = _dot(d.astype(BF16), pw_ref[gi])
        o_ref[:, sl] = (y * ps_ref[:, sl]).astype(o_ref.dtype)
    ext_ref[0:POOL_HALO, :] = ext_ref[tt:tt + POOL_HALO, :]


def _pool_prompt(z, pool_w, pool_scale, *, batch, seq, m, width):
    tt = _tile(seq, 256, 16)
    nt = seq // tt
    gd = width // len(POOL_WINDOWS)
    return pl.pallas_call(
        functools.partial(_pool_prompt_kernel, tt=tt, gd=gd),
        grid=(batch, nt),
        in_specs=[pl.BlockSpec((tt, width), lambda b, t: (b * nt + t, 0)),
                  pl.BlockSpec(pool_w.shape, lambda b, t: (0, 0, 0)),
                  pl.BlockSpec((1, width), lambda b, t: (0, 0))],
        out_specs=[pl.BlockSpec((tt, width), lambda b, t: (b * nt + t, 0)),
                   pl.BlockSpec((None, POOL_HALO, width), lambda b, t: (b, 0, 0))],
        out_shape=[jax.ShapeDtypeStruct((m, width), BF16),
                   jax.ShapeDtypeStruct((batch, POOL_HALO, width), F32)],
        scratch_shapes=[pltpu.VMEM((tt + POOL_HALO, width), F32)],
        compiler_params=_params("parallel", "arbitrary"),
        name="pool_prompt",
    )(z, pool_w, pool_scale)


def _pool_sample_kernel(a_ref, buf_ref, pw_ref, ps_ref, ya_in_ref, o_ref, nbuf_ref, *, width, gd):
    del ya_in_ref
    a = a_ref[...]
    for gi, w in enumerate(POOL_WINDOWS):
        sl = slice(gi * gd, (gi + 1) * gd)
        acc = a[:, sl]
        for s in range(1, w):
            r = POOL_BUF - s
            acc = acc + buf_ref[:, r * width + gi * gd:r * width + (gi + 1) * gd]
        d = acc / float(min(PAST_LEN + 1, w)) - a[:, sl]
        y = _dot(d.astype(BF16), pw_ref[gi])
        o_ref[:, sl] = (y * ps_ref[:, sl]).astype(o_ref.dtype)
    nbuf_ref[:, 0:(POOL_BUF - 1) * width] = buf_ref[:, width:POOL_BUF * width]
    nbuf_ref[:, (POOL_BUF - 1) * width:POOL_BUF * width] = a


def _pool_sample(z, buf_flat, layer, pool_w, pool_scale, ya, *, row0, nsamp, width):
    ns = SAMPLE_BLOCK
    gd = width // len(POOL_WINDOWS)
    rb = row0 // ns
    ya, nbuf = pl.pallas_call(
        functools.partial(_pool_sample_kernel, width=width, gd=gd),
        grid=(nsamp // ns,),
        in_specs=[pl.BlockSpec((ns, width), lambda i: (rb + i, 0)),
                  pl.BlockSpec((None, ns, POOL_BUF * width), lambda i: (layer, i, 0)),
                  pl.BlockSpec(pool_w.shape, lambda i: (0, 0, 0)),
                  pl.BlockSpec((1, width), lambda i: (0, 0)),
                  pl.BlockSpec(memory_space=pl.ANY)],
        out_specs=[pl.BlockSpec((ns, width), lambda i: (rb + i, 0)),
                   pl.BlockSpec((ns, POOL_BUF * width), lambda i: (i, 0))],
        out_shape=[jax.ShapeDtypeStruct(ya.shape, ya.dtype),
                   jax.ShapeDtypeStruct(buf_flat.shape[1:], F32)],
        input_output_aliases={4: 0},
        compiler_params=_params("parallel"),
        name="pool_sample",
    )(z, buf_flat, pool_w, pool_scale, ya)
    return ya, nbuf


def _sgu_prompt_kernel(u_ref, v_ref, lng_ref, lnb_ref, ws_ref, biast_ref, o_ref, *, nchunk, gd):
    c = SGU_CHUNK
    tril = (lax.broadcasted_iota(jnp.int32, (c, c), 0) >= lax.broadcasted_iota(jnp.int32, (c, c), 1)).astype(F32)
    w_m = [(ws_ref[g] * tril).astype(BF16) for g in range(SGU_GROUPS)]
    for ci in range(nchunk):
        rows = slice(ci * c, (ci + 1) * c)
        vn = _row_layer_norm(v_ref[rows, :], lng_ref[...], lnb_ref[...]).astype(BF16)
        for g in range(SGU_GROUPS):
            sl = slice(g * gd, (g + 1) * gd)
            mixed = _dot(w_m[g], vn[:, sl]) + biast_ref[:, g:g + 1]
            o_ref[rows, sl] = (u_ref[rows, sl] * mixed).astype(o_ref.dtype)


def _sgu_prompt(z, ln_g, ln_b, ws, bias_t, *, batch, seq, m, width, u_blk, v_blk):
    assert seq % SGU_CHUNK == 0
    rows = _tile(seq, 512, SGU_CHUNK)
    nt = seq // rows
    gd = width // SGU_GROUPS
    return pl.pallas_call(
        functools.partial(_sgu_prompt_kernel, nchunk=rows // SGU_CHUNK, gd=gd),
        grid=(batch * nt,),
        in_specs=[pl.BlockSpec((rows, width), lambda i: (i, u_blk)),
                  pl.BlockSpec((rows, width), lambda i: (i, v_blk)),
                  pl.BlockSpec((1, width), lambda i: (0, 0)),
                  pl.BlockSpec((1, width), lambda i: (0, 0)),
                  pl.BlockSpec(ws.shape, lambda i: (0, 0, 0)),
                  pl.BlockSpec(bias_t.shape, lambda i: (0, 0))],
        out_specs=pl.BlockSpec((rows, width), lambda i: (i, 0)),
        out_shape=jax.ShapeDtypeStruct((m, width), BF16),
        compiler_params=_params("parallel"),
        name="sgu_prompt",
    )(z, z, ln_g, ln_b, ws, bias_t)


def _sgu_sample_kernel(u_ref, v_ref, lng_ref, lnb_ref, w0_ref, b0_ref, yb_in_ref, o_ref, vn_ref):
    del yb_in_ref
    vn = _row_layer_norm(v_ref[...], lng_ref[...], lnb_ref[...])
    vn_ref[...] = vn
    mixed = _round_bf16(w0_ref[...]) * _round_bf16(vn) + b0_ref[...]
    o_ref[...] = (u_ref[...] * mixed).astype(o_ref.dtype)


def _sgu_sample(z, ln_g, ln_b, w0, b0, yb, *, row0, nsamp, width, u_blk, v_blk):
    ns = SAMPLE_BLOCK
    rb = row0 // ns
    vec = pl.BlockSpec((1, width), lambda i: (0, 0))
    yb, vn = pl.pallas_call(
        _sgu_sample_kernel,
        grid=(nsamp // ns,),
        in_specs=[pl.BlockSpec((ns, width), lambda i: (rb + i, u_blk)),
                  pl.BlockSpec((ns, width), lambda i: (rb + i, v_blk)),
                  vec, vec, vec, vec,
                  pl.BlockSpec(memory_space=pl.ANY)],
        out_specs=[pl.BlockSpec((ns, width), lambda i: (rb + i, 0)),
                   pl.BlockSpec((ns, width), lambda i: (i, 0))],
        out_shape=[jax.ShapeDtypeStruct(yb.shape, yb.dtype),
                   jax.ShapeDtypeStruct((nsamp, width), F32)],
        input_output_aliases={6: 0},
        compiler_params=_params("parallel"),
        name="sgu_sample",
    )(z, z, ln_g, ln_b, w0, b0, yb)
    return yb, vn


def _cumsum_rows(x):
    rows = x.shape[0]
    row = lax.broadcasted_iota(jnp.int32, x.shape, 0)
    s = 1
    while s < rows:
        x = x + jnp.where(row >= s, pltpu.roll(x, s, 0), 0.0)
        s *= 2
    return x


def _gla_out(o, norm_g, r):
    o = o * lax.rsqrt(jnp.mean(o * o, -1, keepdims=True) + EPS) * norm_g
    return o * _silu(r)


def _gla_prompt_kernel(q_ref, k_ref, v01_ref, v23_ref, g_ref, r01_ref, r23_ref, ng_ref, o_ref, s_ref, st_ref, *,
                       nchunk, scale, hk, hv):
    t = pl.program_id(1)
    c = GLA_CHUNK

    @pl.when(t == 0)
    def _():
        st_ref[...] = jnp.zeros(st_ref.shape, F32)

    causal = (lax.broadcasted_iota(jnp.int32, (c, c), 0) >= lax.broadcasted_iota(jnp.int32, (c, c), 1)).astype(F32)
    v_refs = (v01_ref, v23_ref)
    r_refs = (r01_ref, r23_ref)
    for ci in range(nchunk):
        rows = slice(ci * c, (ci + 1) * c)
        for h in range(GLA_HEADS):
            kcols = slice(h * hk, (h + 1) * hk)
            vcols = slice((h % 2) * hv, (h % 2 + 1) * hv)
            ocols = slice(h * hv, (h + 1) * hv)
            b = _cumsum_rows(g_ref[rows, kcols])
            b_end = b[c - 1:c, :]
            k = k_ref[rows, kcols]
            q_in = ((q_ref[rows, kcols] * scale) * jnp.exp(b)).astype(BF16)
            k_in = (k * jnp.exp(-b)).astype(BF16)
            k_dec = (k * jnp.exp(b_end - b)).astype(BF16)
            vb = v_refs[h // 2][rows, vcols].astype(BF16)
            att = _dot_nt(q_in, k_in) * causal
            st = st_ref[h]
            o = _dot(att.astype(BF16), vb) + _dot_nt(q_in, st.astype(BF16))
            st_ref[h] = st * jnp.exp(b_end) + _dot_tn(vb, k_dec)
            o_ref[rows, ocols] = _gla_out(o, ng_ref[:, ocols], r_refs[h // 2][rows, vcols]).astype(o_ref.dtype)

    @pl.when(t == pl.num_programs(1) - 1)
    def _():
        for h in range(GLA_HEADS):
            s_ref[0, h] = st_ref[h].T


def _gla_prompt(z, logf, norm_g, *, batch, seq, m, hk, hv, q_off):
    assert seq % GLA_CHUNK == 0 and GLA_HEADS == 4
    rows = _tile(seq, 256, GLA_CHUNK)
    nt = seq // rows
    heads = GLA_HEADS
    dk, dv2 = heads * hk, 2 * hv
    v_off = q_off + 2 * dk
    assert q_off % dk == 0 and v_off % dv2 == 0

    def cols(width, off):
        return pl.BlockSpec((rows, width), lambda n, t: (n * nt + t, off // width))

    yc, s = pl.pallas_call(
        functools.partial(_gla_prompt_kernel, nchunk=rows // GLA_CHUNK, scale=hk ** -0.5, hk=hk, hv=hv),
        grid=(batch, nt),
        in_specs=[cols(dk, q_off), cols(dk, q_off + dk),
                  cols(dv2, v_off), cols(dv2, v_off + dv2),
                  cols(dk, 0),
                  cols(dv2, v_off + 2 * dv2), cols(dv2, v_off + 3 * dv2),
                  pl.BlockSpec((1, heads * hv), lambda n, t: (0, 0))],
        out_specs=[pl.BlockSpec((rows, heads * hv), lambda n, t: (n * nt + t, 0)),
                   pl.BlockSpec((1, heads, hk, hv), lambda n, t: (n, 0, 0, 0))],
        out_shape=[jax.ShapeDtypeStruct((m, heads * hv), BF16),
                   jax.ShapeDtypeStruct((batch, heads, hk, hv), F32)],
        scratch_shapes=[pltpu.VMEM((heads, hv, hk), F32)],
        compiler_params=_params("parallel", "arbitrary"),
        name="gla_prompt",
    )(z, z, z, z, logf, z, z, norm_g)
    return yc, s


def _gla_sample_kernel(*refs, scale):
    q_ref, k_ref, v_ref, g_ref, r_ref, ng_ref, s_ref, yc_in_ref = refs[:8]
    o_ref, so_ref = refs[-2:]
    del yc_in_ref
    ns = SAMPLE_BLOCK
    g = g_ref[...]
    e = jnp.exp(g)
    k = k_ref[...]
    q_in = _round_bf16((q_ref[...] * scale) * e)
    k_in = _round_bf16(k * jnp.exp(-g))
    k_dec = _round_bf16(k * jnp.exp(g - g))
    vb = _round_bf16(v_ref[...])
    att = jnp.sum(q_in * k_in, -1, keepdims=True)
    hk = g.shape[1]
    cols = jnp.concatenate([e, k_dec, q_in, jnp.zeros((128 - 3 * ns, hk), F32)], axis=0).T
    outs = []
    for j in range(ns):
        s = s_ref[0, j, 0]
        so_ref[0, j, 0] = s * cols[:, j:j + 1] + cols[:, ns + j:ns + j + 1] * vb[j:j + 1, :]
        outs.append(jnp.sum(cols[:, 2 * ns + j:2 * ns + j + 1] * _round_bf16(s), axis=0, keepdims=True))
    o = jnp.concatenate(outs, axis=0) + _round_bf16(att) * vb
    o_ref[...] = _gla_out(o, ng_ref[...], r_ref[...]).astype(o_ref.dtype)


def _gla_sample(z, logf, norm_g, state, layer, yc, prev_out, *, row0, nsamp, hk, hv, q_blk, k_blk, v_blk, r_blk):
    ns = SAMPLE_BLOCK
    rb = row0 // ns
    heads = GLA_HEADS
    st_spec = pl.BlockSpec((1, ns, 1, hk, hv), lambda i, h: (layer, i, h, 0, 0))
    in_specs = [pl.BlockSpec((ns, hk), lambda i, h: (rb + i, q_blk + h)),
                pl.BlockSpec((ns, hk), lambda i, h: (rb + i, k_blk + h)),
                pl.BlockSpec((ns, hv), lambda i, h: (rb + i, v_blk + h)),
                pl.BlockSpec((ns, hk), lambda i, h: (rb + i, h)),
                pl.BlockSpec((ns, hv), lambda i, h: (rb + i, r_blk + h)),
                pl.BlockSpec((1, hv), lambda i, h: (0, h)),
                st_spec,
                pl.BlockSpec(memory_space=pl.ANY)]
    args = [z, z, z, logf, z, norm_g, state, yc]
    aliases = {7: 0}
    if prev_out is not None:
        in_specs.append(pl.BlockSpec(memory_space=pl.ANY))
        args.append(prev_out)
        aliases[8] = 1
    yc, s_out = pl.pallas_call(
        functools.partial(_gla_sample_kernel, scale=hk ** -0.5),
        grid=(nsamp // ns, heads),
        in_specs=in_specs,
        out_specs=[pl.BlockSpec((ns, hv), lambda i, h: (rb + i, h)), st_spec],
        out_shape=[jax.ShapeDtypeStruct(yc.shape, yc.dtype),
                   jax.ShapeDtypeStruct(state.shape, F32)],
        input_output_aliases=aliases,
        compiler_params=_params("parallel", "arbitrary"),
        name="gla_sample",
    )(*args)
    return yc, s_out


def _norm_stream(t, g, b):
    g, b = g.reshape(1, -1), b.reshape(1, -1)
    xb, mu, rstd = _res_ln(t, None, g, b, outputs=("bf16", "stats"))
    return _Normed(t, mu, rstd, g, b), xb


def _ffn(x, xb, w1, w3, w2, layer, g, b, alpha):
    h = _ffn_up(xb, w1, w3, layer)
    down = functools.partial(_matmul, h, w2, layer, tm_target=1040, tn_target=256, name="ffn_down")
    if not isinstance(x, tuple) or isinstance(x, _Normed):
        return _norm_stream(down(residual=x, alpha=alpha, yscale=0.5), g, b)
    y = down()
    ln = functools.partial(_res_ln, g=g, b=b, alpha=alpha, yscale=0.5)
    x_p, x_s = x
    mp, m = x_p.shape[0], y.shape[0]
    outs = ln(x_p, y, out_rows=m)
    return ln(x_s, y, y_row0=mp, out_row0=mp, out_rows=m, into=outs)


def kernel(x_prompt, x_sample, state_pool, state_gla, p_prompt, p_sample, ffa_w1, ffa_w3, ffa_w2, ffb_w1, ffb_w3, ffb_w2, ln_g, ln_b, w_in, b_in, pool_w, pool_scale, sgu_ln_g, sgu_ln_b, sgu_ws, sgu_bias, gla_wa2, gla_ba, gla_norm_g, w_up_a, w_up_b, w_up_c, w_o, pe_w, pe_gate_w):
    depth = ffa_w1.shape[0]
    batch, seq, d = x_prompt.shape
    nsamp, dec_seq, _ = x_sample.shape
    assert dec_seq == 1 and nsamp % SAMPLE_BLOCK == 0
    mp = batch * seq
    m = mp + nsamp
    assert mp % SAMPLE_BLOCK == 0
    alpha = (2.0 * depth) ** 0.25

    pool_wd = pool_w.shape[1] * pool_w.shape[2]
    sgu_wd = sgu_ln_g.shape[1]
    dk = gla_wa2.shape[2]
    dv = gla_norm_g.shape[1]
    hk, hv = dk // GLA_HEADS, dv // GLA_HEADS
    n_main = pool_wd + 2 * sgu_wd + 2 * dk + 2 * dv
    n_gate = n_main + GLA_RANK
    assert w_in.shape[2] == n_gate + 3 * d
    assert pool_wd == sgu_wd
    u_blk, v_blk = pool_wd // sgu_wd, pool_wd // sgu_wd + 1
    q_off = pool_wd + 2 * sgu_wd
    q_blk, k_blk = q_off // hk, (q_off + dk) // hk
    v2_blk, r_blk = (q_off + 2 * dk) // hv, (q_off + 2 * dk + dv) // hv

    x_p2, x_s2 = x_prompt.reshape(mp, d), x_sample.reshape(nsamp, d)
    x = (x_p2, x_s2)
    xb = _stack_cast([x_p2, x_s2], BF16)
    w_in_t = jnp.swapaxes(w_in, 1, 2)

    pool_p, gla_p, pool_s, sgu_s = [], [], [], []
    gla_s = None
    state_pool_flat = state_pool.reshape(depth, nsamp, POOL_BUF * pool_wd)
    for i in range(depth):
        x, xb = _ffn(x, xb, ffa_w1, ffa_w3, ffa_w2, i, ln_g[i, 0], ln_b[i, 0], alpha)

        b_in_i = b_in[i].reshape(1, -1)
        z = _in_proj(xb, w_in_t, b_in_i, i, n_main)
        logf = _gla_logf(xb, w_in_t, b_in_i, i, n_main,
                         jnp.pad(gla_wa2[i].astype(BF16), ((0, 128 - GLA_RANK), (0, 0))),
                         gla_ba[i].reshape(1, dk))

        pw = pool_w[i].astype(BF16)
        ps = pool_scale[i].reshape(1, pool_wd)
        ya, tail = _pool_prompt(z, pw, ps, batch=batch, seq=seq, m=m, width=pool_wd)
        ya, nbuf = _pool_sample(z, state_pool_flat, i, pw, ps, ya, row0=mp, nsamp=nsamp, width=pool_wd)
        pool_p.append(tail[:, POOL_HALO - POOL_BUF:])
        pool_s.append(nbuf.reshape(nsamp, POOL_BUF, pool_wd))

        lng = sgu_ln_g[i].reshape(1, sgu_wd)
        lnb = sgu_ln_b[i].reshape(1, sgu_wd)
        gd = sgu_wd // SGU_GROUPS
        yb = _sgu_prompt(z, lng, lnb, sgu_ws[i], sgu_bias[i].T, batch=batch, seq=seq, m=m, width=sgu_wd,
                         u_blk=u_blk, v_blk=v_blk)
        yb, vn = _sgu_sample(z, lng, lnb,
                             jnp.repeat(sgu_ws[i, :, 0, 0], gd).reshape(1, sgu_wd),
                             jnp.repeat(sgu_bias[i, :, 0], gd).reshape(1, sgu_wd),
                             yb, row0=mp, nsamp=nsamp, width=sgu_wd, u_blk=u_blk, v_blk=v_blk)
        sgu_s.append(vn.reshape(nsamp, 1, sgu_wd))

        ng = gla_norm_g[i].reshape(1, dv)
        blks = dict(hk=hk, hv=hv, q_blk=q_blk, k_blk=k_blk, v_blk=v2_blk, r_blk=r_blk)
        yc, s_p = _gla_prompt(z, logf, ng, batch=batch, seq=seq, m=m, hk=hk, hv=hv, q_off=q_off)
        yc, gla_s = _gla_sample(z, logf, ng, state_gla, i, yc, gla_s, row0=mp, nsamp=nsamp, **blks)
        gla_p.append(s_p)

        w_g = _cast_gate_rows(w_in_t, i, n_gate, d, 3)
        gt = _gate_tile(d)
        b_g = b_in_i[:, n_gate:].reshape(3, d // gt, gt).transpose(1, 0, 2).reshape(1, 3 * d)
        h = _up_gate(xb, ya, yb, yc, w_g, b_g, w_up_a, w_up_b, w_up_c, i)
        t = _matmul(h, w_o, i, tm_target=1664, tn_target=512, name="out_proj", residual=x, alpha=alpha)
        x, xb = _norm_stream(t, ln_g[i, 1], ln_b[i, 1])

        x, xb = _ffn(x, xb, ffb_w1, ffb_w3, ffb_w2, i, ln_g[i, 2], ln_b[i, 2], alpha)

        pb = jnp.concatenate([p_prompt[i].reshape(mp, -1), p_sample[i].reshape(nsamp, -1)], axis=0).astype(BF16)
        t = _pe_embed(xb, pb, pe_gate_w, pe_w, i, x, alpha)
        if i + 1 < depth:
            x, xb = _norm_stream(t, ln_g[i, 3], ln_b[i, 3])
        else:
            ln = functools.partial(_res_ln, t, None, ln_g[i, 3], ln_b[i, 3], outputs=("f32",))
            (x_p,), (x_s,) = ln(rows=mp), ln(rows=nsamp, x_row0=mp)

    return (x_p.reshape(batch, seq, d), x_s.reshape(nsamp, 1, d),
            jnp.stack(pool_p), jnp.stack(gla_p).astype(state_gla.dtype),
            jnp.stack(pool_s), gla_s.astype(state_gla.dtype), jnp.stack(sgu_s))
```

```python
import functools
from typing import Callable, NamedTuple

import jax
import jax.numpy as jnp
from jax import lax
from jax.experimental import pallas as pl
from jax.experimental.pallas import tpu as pltpu

F32 = jnp.float32
BF16 = jnp.bfloat16

POOL_WINDOWS = (2, 4, 8, 16)
POOL_BUF = max(POOL_WINDOWS) - 1
POOL_HALO = 16
SGU_GROUPS = 8
SGU_CHUNK = 128
GLA_HEADS = 4
GLA_RANK = 16
GLA_TAU = 16.0
GLA_CHUNK = 64
EPS = 1e-5
PAST_LEN = 16384
SAMPLE_BLOCK = 8

LANES = 128
V7X_VMEM_LIMIT_BYTES = 60 * 1024 * 1024


def _tile(dim, target, align):
    best = None
    t = align
    while t <= min(dim, target):
        if dim % t == 0:
            best = t
        t += align
    return dim if best is None else best


def _params(*sem):
    return pltpu.CompilerParams(dimension_semantics=sem, vmem_limit_bytes=V7X_VMEM_LIMIT_BYTES)


def _silu(x):
    return x * jax.nn.sigmoid(x)


def _dot(a, b):
    return jnp.dot(a, b, preferred_element_type=F32)


def _dot_nt(a, b):
    return lax.dot_general(a, b, (((1,), (1,)), ((), ())), preferred_element_type=F32)


def _dot_tn(a, b):
    return lax.dot_general(a, b, (((0,), (0,)), ((), ())), preferred_element_type=F32)


def _round_bf16(x):
    return x.astype(BF16).astype(F32)


def _row_stats(v):
    mu = jnp.mean(v, -1, keepdims=True)
    d = v - mu
    return mu, lax.rsqrt(jnp.mean(d * d, -1, keepdims=True) + EPS)


def _apply_norm(v, mu, rstd, g, b):
    return (v - mu) * rstd * g + b


def _row_layer_norm(v, g, b):
    mu, rstd = _row_stats(v)
    return _apply_norm(v, mu, rstd, g, b)


def _residual(x, y, alpha, yscale):
    return alpha * x + (y if yscale is None else yscale * y)


class _Normed(NamedTuple):
    t: jax.Array
    mu: jax.Array
    rstd: jax.Array
    g: jax.Array
    b: jax.Array


def _stream_specs(x, tm, tn):
    tile = pl.BlockSpec((tm, tn), lambda i, j: (i, j))
    if not isinstance(x, _Normed):
        return [tile], [x]
    stat = pl.BlockSpec((tm, LANES), lambda i, j: (i, 0))
    vec = pl.BlockSpec((1, tn), lambda i, j: (0, j))
    return [tile, stat, stat, vec, vec], list(x)


def _stream_tile(refs):
    if len(refs) == 1:
        return refs[0][...]
    t_ref, mu_ref, rstd_ref, g_ref, b_ref = refs
    return _apply_norm(t_ref[...], mu_ref[:, 0:1], rstd_ref[:, 0:1], g_ref[...], b_ref[...])


def _w_spec(w, layer, k, tn):
    assert w.shape[-2] == k
    if w.ndim == 2:
        return pl.BlockSpec((k, tn), lambda i, j: (0, j))
    return pl.BlockSpec((None, k, tn), lambda i, j: (layer, 0, j))


def _resident_rows(tm, k):
    return pl.BlockSpec((tm, k), lambda i, j: (i, 0), pipeline_mode=pl.Buffered(1))


class _SideCast(NamedTuple):
    src: jax.Array
    n_slabs: int
    in_block: tuple
    in_index: Callable
    out_block: tuple
    out_index: Callable
    out_shape: jax.ShapeDtypeStruct


def _cast_kernel(w_ref, o_ref):
    o_ref[...] = w_ref[...].astype(o_ref.dtype)


def _run_side_cast(side):
    return pl.pallas_call(
        _cast_kernel,
        grid=(side.n_slabs,),
        in_specs=[pl.BlockSpec(side.in_block, side.in_index)],
        out_specs=pl.BlockSpec(side.out_block, side.out_index),
        out_shape=side.out_shape,
        compiler_params=_params("parallel"),
        name="cast_rows",
    )(side.src)


def _ffn_up_tiles(m, n):
    return _tile(m, 2080, 16), _tile(n, 256, 128)


def _ffn_up_steps(m, n):
    tm, tn = _ffn_up_tiles(m, n)
    return (m // tm) * (n // tn)


def _glu_kernel(x_ref, w1_ref, w3_ref, *refs, side_slabs, steps):
    n_side = len(side_slabs)
    x = x_ref[...]
    a = _dot(x, w1_ref[...].astype(BF16))
    b = _dot(x, w3_ref[...].astype(BF16))
    refs[n_side][...] = (_silu(a) * b).astype(refs[n_side].dtype)
    step = pl.program_id(0) * pl.num_programs(1) + pl.program_id(1)
    for n_slabs, src_ref, dst_ref in zip(side_slabs, refs[:n_side], refs[n_side + 1:]):
        def cast(src_ref=src_ref, dst_ref=dst_ref):
            dst_ref[...] = src_ref[...].astype(dst_ref.dtype)
        if n_slabs == steps:
            cast()
        else:
            pl.when(step < n_slabs)(cast)


def _ffn_up(xb, w1, w3, layer, sides=()):
    m, k = xb.shape
    n = w1.shape[-1]
    tm, tn = _ffn_up_tiles(m, n)
    nj = n // tn
    steps = _ffn_up_steps(m, n)
    riding = [s for s in sides if s.n_slabs <= steps]

    def slab_spec(block, index, n_slabs):
        return pl.BlockSpec(block, lambda i, j: index(jnp.minimum(i * nj + j, n_slabs - 1)))

    h, *casts = pl.pallas_call(
        functools.partial(_glu_kernel, side_slabs=tuple(s.n_slabs for s in riding), steps=steps),
        grid=(m // tm, nj),
        in_specs=[_resident_rows(tm, k),
                  _w_spec(w1, layer, k, tn),
                  _w_spec(w3, layer, k, tn),
                  *[slab_spec(s.in_block, s.in_index, s.n_slabs) for s in riding]],
        out_specs=[pl.BlockSpec((tm, tn), lambda i, j: (i, j)),
                   *[slab_spec(s.out_block, s.out_index, s.n_slabs) for s in riding]],
        out_shape=[jax.ShapeDtypeStruct((m, n), BF16), *[s.out_shape for s in riding]],
        compiler_params=_params("parallel", "arbitrary"),
        name="ffn_up",
    )(xb, w1, w3, *[s.src for s in riding])
    casts = iter(casts)
    return h, [next(casts) if s.n_slabs <= steps else _run_side_cast(s) for s in sides]


def _mm_kernel(x_ref, w_ref, o_ref):
    o_ref[...] = _dot(x_ref[...], w_ref[...].astype(BF16)).astype(o_ref.dtype)


def _mm_nt_bias_kernel(x_ref, wt_ref, b_ref, o_ref):
    o_ref[...] = (_dot_nt(x_ref[...], wt_ref[...].astype(BF16)) + b_ref[...]).astype(o_ref.dtype)


def _cast_into_kernel(x_ref, *refs):
    refs[-1][...] = x_ref[...].astype(refs[-1].dtype)


def _stack_cast(parts, dtype):
    d = parts[0].shape[1]
    m = sum(p.shape[0] for p in parts)
    out, row0 = None, 0
    for p in parts:
        rows = p.shape[0]
        tm = _tile(rows, 512, 16)
        assert row0 % tm == 0
        earlier = [] if out is None else [out]
        out = pl.pallas_call(
            _cast_into_kernel,
            grid=(rows // tm,),
            in_specs=[pl.BlockSpec((tm, d), lambda i: (i, 0))] + [pl.BlockSpec(memory_space=pl.ANY)] * len(earlier),
            out_specs=pl.BlockSpec((tm, d), lambda i, blk0=row0 // tm: (blk0 + i, 0)),
            out_shape=jax.ShapeDtypeStruct((m, d), dtype),
            input_output_aliases={1: 0} if earlier else {},
            compiler_params=_params("parallel"),
            name="stack_cast",
        )(p, *earlier)
        row0 += rows
    return out


def _gate_tile(d):
    return _tile(d, 256, 128)


def _gate_rows_cast(w, layer, row0, d, branches):
    _, _, k = w.shape
    tr = _gate_tile(d)
    nb = d // tr
    assert row0 % 8 == 0
    return _SideCast(
        src=w, n_slabs=branches * nb,
        in_block=(None, pl.Element(tr), pl.Element(k)),
        in_index=lambda s: (layer, pl.multiple_of(row0 + s * tr, 8), 0),
        out_block=(tr, k),
        out_index=lambda s: ((s % nb) * branches + s // nb, 0),
        out_shape=jax.ShapeDtypeStruct((branches * d, k), BF16))


def _layer_cast(w, layer, max_slabs):
    _, k, n = w.shape
    slab = next((r for r in range(16, k + 1, 16) if k % r == 0 and k // r <= max_slabs), k)
    return _SideCast(
        src=w, n_slabs=k // slab,
        in_block=(None, slab, n), in_index=lambda s: (layer, s, 0),
        out_block=(slab, n), out_index=lambda s: (s, 0),
        out_shape=jax.ShapeDtypeStruct((k, n), BF16))


def _mm_res_kernel(x_ref, w_ref, *refs, alpha, yscale):
    *r_refs, o_ref = refs
    o_ref[...] = _residual(_stream_tile(r_refs), _dot(x_ref[...], w_ref[...].astype(BF16)), alpha, yscale)


def _matmul(xb, w, layer, *, tm_target, tn_target, name, residual=None, alpha=None, yscale=None):
    m, k = xb.shape
    n = w.shape[-1]
    tm = _tile(m, tm_target, 16)
    tn = _tile(n, tn_target, 128)
    tile = pl.BlockSpec((tm, tn), lambda i, j: (i, j))
    in_specs = [_resident_rows(tm, k), _w_spec(w, layer, k, tn)]
    args = [xb, w]
    kern = _mm_kernel
    if residual is not None:
        r_specs, r_args = _stream_specs(residual, tm, tn)
        in_specs += r_specs
        args += r_args
        kern = functools.partial(_mm_res_kernel, alpha=alpha, yscale=yscale)
    return pl.pallas_call(
        kern,
        grid=(m // tm, n // tn),
        in_specs=in_specs,
        out_specs=tile,
        out_shape=jax.ShapeDtypeStruct((m, n), F32),
        compiler_params=_params("parallel", "arbitrary"),
        name=name,
    )(*args)


def _in_proj(xb, w_t, bias, layer, n):
    m, k = xb.shape
    tm = _tile(m, 1664, 16)
    tn = _tile(n, 512, 128)
    return pl.pallas_call(
        _mm_nt_bias_kernel,
        grid=(m // tm, n // tn),
        in_specs=[_resident_rows(tm, k),
                  pl.BlockSpec((None, tn, k), lambda i, j: (layer, j, 0)),
                  pl.BlockSpec((1, tn), lambda i, j: (0, j))],
        out_specs=pl.BlockSpec((tm, tn), lambda i, j: (i, j)),
        out_shape=jax.ShapeDtypeStruct((m, n), F32),
        compiler_params=_params("parallel", "arbitrary"),
        name="in_proj",
    )(xb, w_t, bias)


def _ln_kernel(*refs, alpha, yscale, n_src, n_alias, outputs):
    srcs, (g_ref, b_ref), outs = refs[:n_src], refs[n_src:n_src + 2], list(refs[n_src + 2 + n_alias:])
    t = srcs[0][...] if n_src == 1 else _residual(srcs[0][...], srcs[1][...], alpha, yscale)
    mu, rstd = _row_stats(t)
    o = _apply_norm(t, mu, rstd, g_ref[...], b_ref[...])
    for kind in outputs:
        if kind == "f32":
            outs.pop(0)[...] = o
        elif kind == "bf16":
            outs.pop(0)[...] = o.astype(BF16)
        else:
            for stat in (mu, rstd):
                ref = outs.pop(0)
                ref[...] = jnp.broadcast_to(stat, ref.shape)


def _res_ln(x, y, g, b, *, alpha=None, yscale=None, rows=None, x_row0=0, y_row0=0, out_row0=0, out_rows=None,
            outputs=("f32", "bf16"), into=None):
    d = x.shape[1]
    rows = x.shape[0] if rows is None else rows
    out_rows = rows if out_rows is None else out_rows
    tm = _tile(rows, 416, 16)
    assert x_row0 % tm == 0 and y_row0 % tm == 0 and out_row0 % tm == 0

    def at(row0, width=d):
        return pl.BlockSpec((tm, width), lambda i: (row0 // tm + i, 0))

    vec = pl.BlockSpec((1, d), lambda i: (0, 0))
    out_shape, out_specs = [], []
    for kind in outputs:
        if kind == "stats":
            out_shape += [jax.ShapeDtypeStruct((out_rows, LANES), F32)] * 2
            out_specs += [at(out_row0, LANES)] * 2
        else:
            out_shape.append(jax.ShapeDtypeStruct((out_rows, d), F32 if kind == "f32" else BF16))
            out_specs.append(at(out_row0))
    srcs = [(x, x_row0)] + ([] if y is None else [(y, y_row0)])
    in_specs = [at(r0) for _, r0 in srcs] + [vec, vec]
    args = [a for a, _ in srcs] + [g.reshape(1, d), b.reshape(1, d)]
    aliases = {}
    if into is not None:
        in_specs += [pl.BlockSpec(memory_space=pl.ANY)] * len(into)
        aliases = {len(args) + n: n for n in range(len(into))}
        args += list(into)
    return pl.pallas_call(
        functools.partial(_ln_kernel, alpha=alpha, yscale=yscale, n_src=len(srcs), n_alias=len(aliases),
                          outputs=outputs),
        grid=(rows // tm,),
        in_specs=in_specs,
        out_specs=out_specs,
        out_shape=out_shape,
        input_output_aliases=aliases,
        compiler_params=_params("parallel"),
        name="res_ln",
    )(*args)


def _log_sigmoid(x):
    return jnp.minimum(x, 0.0) - jnp.log1p(jnp.exp(-jnp.abs(x)))


def _logf_kernel(x_ref, wt_ref, bl_ref, wa_ref, ba_ref, o_ref):
    a_lr = _dot_nt(x_ref[...], wt_ref[...].astype(BF16)) + bl_ref[...]
    a_lr = jnp.where(lax.broadcasted_iota(jnp.int32, a_lr.shape, 1) < GLA_RANK, a_lr, 0.0)
    t = _dot(a_lr.astype(BF16), wa_ref[...]) + ba_ref[...]
    o_ref[...] = _log_sigmoid(t) / GLA_TAU


def _gla_logf(xb, w_in_t, b_in, layer, col0, wa2_pad, ba):
    m, k = xb.shape
    lane, dk = wa2_pad.shape
    assert col0 % lane == 0
    tm = _tile(m, 1040, 16)
    return pl.pallas_call(
        _logf_kernel,
        grid=(m // tm,),
        in_specs=[pl.BlockSpec((tm, k), lambda i: (i, 0)),
                  pl.BlockSpec((None, lane, k), lambda i: (layer, col0 // lane, 0)),
                  pl.BlockSpec((1, lane), lambda i: (0, col0 // lane)),
                  pl.BlockSpec((lane, dk), lambda i: (0, 0)),
                  pl.BlockSpec((1, dk), lambda i: (0, 0))],
        out_specs=pl.BlockSpec((tm, dk), lambda i: (i, 0)),
        out_shape=jax.ShapeDtypeStruct((m, dk), F32),
        compiler_params=_params("parallel"),
        name="gla_logf",
    )(xb, w_in_t, b_in, wa2_pad, ba)


def _upgate_kernel(x_ref, ya_ref, yb_ref, yc_ref, wg_ref, bg_ref, wa_ref, wb_ref, wc_ref, o_ref):
    tn = o_ref.shape[1]
    gates = jax.nn.sigmoid(_dot_nt(x_ref[...], wg_ref[...]) + bg_ref[...])
    acc = gates[:, :tn] * _dot(ya_ref[...], wa_ref[...].astype(BF16))
    acc = acc + gates[:, tn:2 * tn] * _dot(yb_ref[...], wb_ref[...].astype(BF16))
    acc = acc + gates[:, 2 * tn:] * _dot(yc_ref[...], wc_ref[...].astype(BF16))
    o_ref[...] = acc.astype(o_ref.dtype)


def _up_gate(xb, ya, yb, yc, w_g, b_g, w_up_a, w_up_b, w_up_c, layer):
    m, k = xb.shape
    d = w_up_a.shape[-1]
    tm = _tile(m, 1040, 16)
    tn = _gate_tile(d)

    def rows(width):
        return _resident_rows(tm, width)

    def up_cols(width):
        return pl.BlockSpec((None, width, tn), lambda i, j: (layer, 0, j))

    return pl.pallas_call(
        _upgate_kernel,
        grid=(m // tm, d // tn),
        in_specs=[rows(k), rows(ya.shape[1]), rows(yb.shape[1]), rows(yc.shape[1]),
                  pl.BlockSpec((3 * tn, k), lambda i, j: (j, 0)),
                  pl.BlockSpec((1, 3 * tn), lambda i, j: (0, j)),
                  up_cols(ya.shape[1]), up_cols(yb.shape[1]), up_cols(yc.shape[1])],
        out_specs=pl.BlockSpec((tm, tn), lambda i, j: (i, j)),
        out_shape=jax.ShapeDtypeStruct((m, d), BF16),
        compiler_params=_params("parallel", "arbitrary"),
        name="up_gate",
    )(xb, ya, yb, yc, w_g, b_g, w_up_a, w_up_b, w_up_c)


def _pe_kernel(x_ref, p_ref, wg_ref, wp_ref, *refs, alpha):
    *r_refs, o_ref = refs
    y = jax.nn.sigmoid(_dot(x_ref[...], wg_ref[...].astype(BF16))) * _dot(p_ref[...], wp_ref[...].astype(BF16))
    o_ref[...] = _residual(_stream_tile(r_refs), y, alpha, None)


def _pe_embed(xb, pb, w_gate, w_p, layer, residual, alpha):
    m, k = xb.shape
    d = w_gate.shape[-1]
    kp = pb.shape[1]
    tm = _tile(m, 1664, 16)
    tn = _tile(d, 512, 128)
    r_specs, r_args = _stream_specs(residual, tm, tn)
    return pl.pallas_call(
        functools.partial(_pe_kernel, alpha=alpha),
        grid=(m // tm, d // tn),
        in_specs=[_resident_rows(tm, k),
                  _resident_rows(tm, kp),
                  _w_spec(w_gate, layer, k, tn),
                  _w_spec(w_p, layer, kp, tn),
                  *r_specs],
        out_specs=pl.BlockSpec((tm, tn), lambda i, j: (i, j)),
        out_shape=jax.ShapeDtypeStruct((m, d), F32),
        compiler_params=_params("parallel", "arbitrary"),
        name="pe_embed",
    )(xb, pb, w_gate, w_p, *r_args)


def _pool_prompt_kernel(a_ref, pw_ref, ps_ref, o_ref, tail_ref, ext_ref, *, tt, gd):
    t = pl.program_id(1)

    @pl.when(t == pl.num_programs(1) - 1)
    def _():
        tail_ref[...] = a_ref[tt - POOL_HALO:tt, :]

    @pl.when(t == 0)
    def _():
        ext_ref[0:POOL_HALO, :] = jnp.zeros((POOL_HALO, ext_ref.shape[1]), F32)

    a = a_ref[...]
    ext_ref[POOL_HALO:POOL_HALO + tt, :] = a
    pos = t * tt + lax.broadcasted_iota(jnp.int32, (tt, 1), 0)
    for gi, w in enumerate(POOL_WINDOWS):
        sl = slice(gi * gd, (gi + 1) * gd)
        acc = a[:, sl]
        for s in range(1, w):
            acc = acc + ext_ref[POOL_HALO - s:POOL_HALO - s + tt, sl]
        cnt = jnp.minimum(pos + 1, w).astype(F32)
        d = acc / cnt - a[:, sl]
        y = _dot(d.astype(BF16), pw_ref[gi])
        o_ref[:, sl] = (y * ps_ref[:, sl]).astype(o_ref.dtype)
    ext_ref[0:POOL_HALO, :] = ext_ref[tt:tt + POOL_HALO, :]


def _pool_prompt(z, pool_w, pool_scale, *, batch, seq, m, width):
    tt = _tile(seq, 256, 16)
    nt = seq // tt
    gd = width // len(POOL_WINDOWS)
    return pl.pallas_call(
        functools.partial(_pool_prompt_kernel, tt=tt, gd=gd),
        grid=(batch, nt),
        in_specs=[pl.BlockSpec((tt, width), lambda b, t: (b * nt + t, 0)),
                  pl.BlockSpec(pool_w.shape, lambda b, t: (0, 0, 0)),
                  pl.BlockSpec((1, width), lambda b, t: (0, 0))],
        out_specs=[pl.BlockSpec((tt, width), lambda b, t: (b * nt + t, 0)),
                   pl.BlockSpec((None, POOL_HALO, width), lambda b, t: (b, 0, 0))],
        out_shape=[jax.ShapeDtypeStruct((m, width), BF16),
                   jax.ShapeDtypeStruct((batch, POOL_HALO, width), F32)],
        scratch_shapes=[pltpu.VMEM((tt + POOL_HALO, width), F32)],
        compiler_params=_params("parallel", "arbitrary"),
        name="pool_prompt",
    )(z, pool_w, pool_scale)


def _pool_sample_kernel(a_ref, buf_ref, pw_ref, ps_ref, ya_in_ref, o_ref, nbuf_ref, *, width, gd):
    del ya_in_ref
    a = a_ref[...]
    for gi, w in enumerate(POOL_WINDOWS):
        sl = slice(gi * gd, (gi + 1) * gd)
        acc = a[:, sl]
        for s in range(1, w):
            r = POOL_BUF - s
            acc = acc + buf_ref[:, r * width + gi * gd:r * width + (gi + 1) * gd]
        d = acc / float(min(PAST_LEN + 1, w)) - a[:, sl]
        y = _dot(d.astype(BF16), pw_ref[gi])
        o_ref[:, sl] = (y * ps_ref[:, sl]).astype(o_ref.dtype)
    nbuf_ref[:, 0:(POOL_BUF - 1) * width] = buf_ref[:, width:POOL_BUF * width]
    nbuf_ref[:, (POOL_BUF - 1) * width:POOL_BUF * width] = a


def _pool_sample(z, buf_flat, layer, pool_w, pool_scale, ya, *, row0, nsamp, width):
    ns = SAMPLE_BLOCK
    gd = width // len(POOL_WINDOWS)
    rb = row0 // ns
    ya, nbuf = pl.pallas_call(
        functools.partial(_pool_sample_kernel, width=width, gd=gd),
        grid=(nsamp // ns,),
        in_specs=[pl.BlockSpec((ns, width), lambda i: (rb + i, 0)),
                  pl.BlockSpec((None, ns, POOL_BUF * width), lambda i: (layer, i, 0)),
                  pl.BlockSpec(pool_w.shape, lambda i: (0, 0, 0)),
                  pl.BlockSpec((1, width), lambda i: (0, 0)),
                  pl.BlockSpec(memory_space=pl.ANY)],
        out_specs=[pl.BlockSpec((ns, width), lambda i: (rb + i, 0)),
                   pl.BlockSpec((ns, POOL_BUF * width), lambda i: (i, 0))],
        out_shape=[jax.ShapeDtypeStruct(ya.shape, ya.dtype),
                   jax.ShapeDtypeStruct(buf_flat.shape[1:], F32)],
        input_output_aliases={4: 0},
        compiler_params=_params("parallel"),
        name="pool_sample",
    )(z, buf_flat, pool_w, pool_scale, ya)
    return ya, nbuf


def _sgu_prompt_kernel(u_ref, v_ref, lng_ref, lnb_ref, ws_ref, biast_ref, o_ref, *, nchunk, gd):
    c = SGU_CHUNK
    tril = (lax.broadcasted_iota(jnp.int32, (c, c), 0) >= lax.broadcasted_iota(jnp.int32, (c, c), 1)).astype(F32)
    w_m = [(ws_ref[g] * tril).astype(BF16) for g in range(SGU_GROUPS)]
    for ci in range(nchunk):
        rows = slice(ci * c, (ci + 1) * c)
        vn = _row_layer_norm(v_ref[rows, :], lng_ref[...], lnb_ref[...]).astype(BF16)
        for g in range(SGU_GROUPS):
            sl = slice(g * gd, (g + 1) * gd)
            mixed = _dot(w_m[g], vn[:, sl]) + biast_ref[:, g:g + 1]
            o_ref[rows, sl] = (u_ref[rows, sl] * mixed).astype(o_ref.dtype)


def _sgu_prompt(z, ln_g, ln_b, ws, bias_t, *, batch, seq, m, width, u_blk, v_blk):
    assert seq % SGU_CHUNK == 0
    rows = _tile(seq, 512, SGU_CHUNK)
    nt = seq // rows
    gd = width // SGU_GROUPS
    return pl.pallas_call(
        functools.partial(_sgu_prompt_kernel, nchunk=rows // SGU_CHUNK, gd=gd),
        grid=(batch * nt,),
        in_specs=[pl.BlockSpec((rows, width), lambda i: (i, u_blk)),
                  pl.BlockSpec((rows, width), lambda i: (i, v_blk)),
                  pl.BlockSpec((1, width), lambda i: (0, 0)),
                  pl.BlockSpec((1, width), lambda i: (0, 0)),
                  pl.BlockSpec(ws.shape, lambda i: (0, 0, 0)),
                  pl.BlockSpec(bias_t.shape, lambda i: (0, 0))],
        out_specs=pl.BlockSpec((rows, width), lambda i: (i, 0)),
        out_shape=jax.ShapeDtypeStruct((m, width), BF16),
        compiler_params=_params("parallel"),
        name="sgu_prompt",
    )(z, z, ln_g, ln_b, ws, bias_t)


def _sgu_sample_kernel(u_ref, v_ref, lng_ref, lnb_ref, w0_ref, b0_ref, yb_in_ref, o_ref, vn_ref):
    del yb_in_ref
    vn = _row_layer_norm(v_ref[...], lng_ref[...], lnb_ref[...])
    vn_ref[...] = vn
    mixed = _round_bf16(w0_ref[...]) * _round_bf16(vn) + b0_ref[...]
    o_ref[...] = (u_ref[...] * mixed).astype(o_ref.dtype)


def _sgu_sample(z, ln_g, ln_b, w0, b0, yb, *, row0, nsamp, width, u_blk, v_blk):
    ns = SAMPLE_BLOCK
    rb = row0 // ns
    vec = pl.BlockSpec((1, width), lambda i: (0, 0))
    yb, vn = pl.pallas_call(
        _sgu_sample_kernel,
        grid=(nsamp // ns,),
        in_specs=[pl.BlockSpec((ns, width), lambda i: (rb + i, u_blk)),
                  pl.BlockSpec((ns, width), lambda i: (rb + i, v_blk)),
                  vec, vec, vec, vec,
                  pl.BlockSpec(memory_space=pl.ANY)],
        out_specs=[pl.BlockSpec((ns, width), lambda i: (rb + i, 0)),
                   pl.BlockSpec((ns, width), lambda i: (i, 0))],
        out_shape=[jax.ShapeDtypeStruct(yb.shape, yb.dtype),
                   jax.ShapeDtypeStruct((nsamp, width), F32)],
        input_output_aliases={6: 0},
        compiler_params=_params("parallel"),
        name="sgu_sample",
    )(z, z, ln_g, ln_b, w0, b0, yb)
    return yb, vn


def _cumsum_rows(x):
    rows = x.shape[0]
    row = lax.broadcasted_iota(jnp.int32, x.shape, 0)
    s = 1
    while s < rows:
        x = x + jnp.where(row >= s, pltpu.roll(x, s, 0), 0.0)
        s *= 2
    return x


def _gla_out(o, norm_g, r):
    o = o * lax.rsqrt(jnp.mean(o * o, -1, keepdims=True) + EPS) * norm_g
    return o * _silu(r)


def _gla_prompt_kernel(q_ref, k_ref, v01_ref, v23_ref, g_ref, r01_ref, r23_ref, ng_ref, o_ref, s_ref, st_ref, *,
                       nchunk, scale, hk, hv):
    t = pl.program_id(1)
    c = GLA_CHUNK

    @pl.when(t == 0)
    def _():
        st_ref[...] = jnp.zeros(st_ref.shape, F32)

    causal = (lax.broadcasted_iota(jnp.int32, (c, c), 0) >= lax.broadcasted_iota(jnp.int32, (c, c), 1)).astype(F32)
    v_refs = (v01_ref, v23_ref)
    r_refs = (r01_ref, r23_ref)
    for ci in range(nchunk):
        rows = slice(ci * c, (ci + 1) * c)
        for h in range(GLA_HEADS):
            kcols = slice(h * hk, (h + 1) * hk)
            vcols = slice((h % 2) * hv, (h % 2 + 1) * hv)
            ocols = slice(h * hv, (h + 1) * hv)
            b = _cumsum_rows(g_ref[rows, kcols])
            b_end = b[c - 1:c, :]
            k = k_ref[rows, kcols]
            q_in = ((q_ref[rows, kcols] * scale) * jnp.exp(b)).astype(BF16)
            k_in = (k * jnp.exp(-b)).astype(BF16)
            k_dec = (k * jnp.exp(b_end - b)).astype(BF16)
            vb = v_refs[h // 2][rows, vcols].astype(BF16)
            att = _dot_nt(q_in, k_in) * causal
            st = st_ref[h]
            o = _dot(att.astype(BF16), vb) + _dot_nt(q_in, st.astype(BF16))
            st_ref[h] = st * jnp.exp(b_end) + _dot_tn(vb, k_dec)
            o_ref[rows, ocols] = _gla_out(o, ng_ref[:, ocols], r_refs[h // 2][rows, vcols]).astype(o_ref.dtype)

    @pl.when(t == pl.num_programs(1) - 1)
    def _():
        for h in range(GLA_HEADS):
            s_ref[0, h] = st_ref[h].T


def _gla_prompt(z, logf, norm_g, *, batch, seq, m, hk, hv, q_off):
    assert seq % GLA_CHUNK == 0 and GLA_HEADS == 4
    rows = _tile(seq, 256, GLA_CHUNK)
    nt = seq // rows
    heads = GLA_HEADS
    dk, dv2 = heads * hk, 2 * hv
    v_off = q_off + 2 * dk
    assert q_off % dk == 0 and v_off % dv2 == 0

    def cols(width, off):
        return pl.BlockSpec((rows, width), lambda n, t: (n * nt + t, off // width))

    yc, s = pl.pallas_call(
        functools.partial(_gla_prompt_kernel, nchunk=rows // GLA_CHUNK, scale=hk ** -0.5, hk=hk, hv=hv),
        grid=(batch, nt),
        in_specs=[cols(dk, q_off), cols(dk, q_off + dk),
                  cols(dv2, v_off), cols(dv2, v_off + dv2),
                  cols(dk, 0),
                  cols(dv2, v_off + 2 * dv2), cols(dv2, v_off + 3 * dv2),
                  pl.BlockSpec((1, heads * hv), lambda n, t: (0, 0))],
        out_specs=[pl.BlockSpec((rows, heads * hv), lambda n, t: (n * nt + t, 0)),
                   pl.BlockSpec((1, heads, hk, hv), lambda n, t: (n, 0, 0, 0))],
        out_shape=[jax.ShapeDtypeStruct((m, heads * hv), BF16),
                   jax.ShapeDtypeStruct((batch, heads, hk, hv), F32)],
        scratch_shapes=[pltpu.VMEM((heads, hv, hk), F32)],
        compiler_params=_params("parallel", "arbitrary"),
        name="gla_prompt",
    )(z, z, z, z, logf, z, z, norm_g)
    return yc, s


def _gla_sample_kernel(*refs, scale):
    q_ref, k_ref, v_ref, g_ref, r_ref, ng_ref, s_ref, yc_in_ref = refs[:8]
    o_ref, so_ref = refs[-2:]
    del yc_in_ref
    ns = SAMPLE_BLOCK
    g = g_ref[...]
    e = jnp.exp(g)
    k = k_ref[...]
    q_in = _round_bf16((q_ref[...] * scale) * e)
    k_in = _round_bf16(k * jnp.exp(-g))
    k_dec = _round_bf16(k * jnp.exp(g - g))
    vb = _round_bf16(v_ref[...])
    att = jnp.sum(q_in * k_in, -1, keepdims=True)
    hk = g.shape[1]
    cols = jnp.concatenate([e, k_dec, q_in, jnp.zeros((128 - 3 * ns, hk), F32)], axis=0).T
    outs = []
    for j in range(ns):
        s = s_ref[0, j, 0]
        so_ref[0, j, 0] = s * cols[:, j:j + 1] + cols[:, ns + j:ns + j + 1] * vb[j:j + 1, :]
        outs.append(jnp.sum(cols[:, 2 * ns + j:2 * ns + j + 1] * _round_bf16(s), axis=0, keepdims=True))
    o = jnp.concatenate(outs, axis=0) + _round_bf16(att) * vb
    o_ref[...] = _gla_out(o, ng_ref[...], r_ref[...]).astype(o_ref.dtype)


def _gla_sample(z, logf, norm_g, state, layer, yc, prev_out, *, row0, nsamp, hk, hv, q_blk, k_blk, v_blk, r_blk):
    ns = SAMPLE_BLOCK
    rb = row0 // ns
    heads = GLA_HEADS
    st_spec = pl.BlockSpec((1, ns, 1, hk, hv), lambda i, h: (layer, i, h, 0, 0))
    in_specs = [pl.BlockSpec((ns, hk), lambda i, h: (rb + i, q_blk + h)),
                pl.BlockSpec((ns, hk), lambda i, h: (rb + i, k_blk + h)),
                pl.BlockSpec((ns, hv), lambda i, h: (rb + i, v_blk + h)),
                pl.BlockSpec((ns, hk), lambda i, h: (rb + i, h)),
                pl.BlockSpec((ns, hv), lambda i, h: (rb + i, r_blk + h)),
                pl.BlockSpec((1, hv), lambda i, h: (0, h)),
                st_spec,
                pl.BlockSpec(memory_space=pl.ANY)]
    args = [z, z, z, logf, z, norm_g, state, yc]
    aliases = {7: 0}
    if prev_out is not None:
        in_specs.append(pl.BlockSpec(memory_space=pl.ANY))
        args.append(prev_out)
        aliases[8] = 1
    yc, s_out = pl.pallas_call(
        functools.partial(_gla_sample_kernel, scale=hk ** -0.5),
        grid=(nsamp // ns, heads),
        in_specs=in_specs,
        out_specs=[pl.BlockSpec((ns, hv), lambda i, h: (rb + i, h)), st_spec],
        out_shape=[jax.ShapeDtypeStruct(yc.shape, yc.dtype),
                   jax.ShapeDtypeStruct(state.shape, F32)],
        input_output_aliases=aliases,
        compiler_params=_params("parallel", "arbitrary"),
        name="gla_sample",
    )(*args)
    return yc, s_out


def _norm_stream(t, g, b):
    g, b = g.reshape(1, -1), b.reshape(1, -1)
    xb, mu, rstd = _res_ln(t, None, g, b, outputs=("bf16", "stats"))
    return _Normed(t, mu, rstd, g, b), xb


def _ffn(x, xb, w1, w3, w2, layer, g, b, alpha, sides=()):
    w2_cast = _layer_cast(w2, layer, max_slabs=_ffn_up_steps(xb.shape[0], w1.shape[-1]))
    h, (w2b, *extra) = _ffn_up(xb, w1, w3, layer, [w2_cast, *sides])
    down = functools.partial(_matmul, h, w2b, None, tm_target=1040, tn_target=512, name="ffn_down")
    if not isinstance(x, tuple) or isinstance(x, _Normed):
        return (*_norm_stream(down(residual=x, alpha=alpha, yscale=0.5), g, b), extra)
    y = down()
    ln = functools.partial(_res_ln, g=g, b=b, alpha=alpha, yscale=0.5)
    x_p, x_s = x
    mp, m = x_p.shape[0], y.shape[0]
    outs = ln(x_p, y, out_rows=m)
    return (*ln(x_s, y, y_row0=mp, out_row0=mp, out_rows=m, into=outs), extra)


def kernel(x_prompt, x_sample, state_pool, state_gla, p_prompt, p_sample, ffa_w1, ffa_w3, ffa_w2, ffb_w1, ffb_w3, ffb_w2, ln_g, ln_b, w_in, b_in, pool_w, pool_scale, sgu_ln_g, sgu_ln_b, sgu_ws, sgu_bias, gla_wa2, gla_ba, gla_norm_g, w_up_a, w_up_b, w_up_c, w_o, pe_w, pe_gate_w):
    depth = ffa_w1.shape[0]
    batch, seq, d = x_prompt.shape
    nsamp, dec_seq, _ = x_sample.shape
    assert dec_seq == 1 and nsamp % SAMPLE_BLOCK == 0
    mp = batch * seq
    m = mp + nsamp
    assert mp % SAMPLE_BLOCK == 0
    alpha = (2.0 * depth) ** 0.25

    pool_wd = pool_w.shape[1] * pool_w.shape[2]
    sgu_wd = sgu_ln_g.shape[1]
    dk = gla_wa2.shape[2]
    dv = gla_norm_g.shape[1]
    hk, hv = dk // GLA_HEADS, dv // GLA_HEADS
    n_main = pool_wd + 2 * sgu_wd + 2 * dk + 2 * dv
    n_gate = n_main + GLA_RANK
    assert w_in.shape[2] == n_gate + 3 * d
    assert pool_wd == sgu_wd
    u_blk, v_blk = pool_wd // sgu_wd, pool_wd // sgu_wd + 1
    q_off = pool_wd + 2 * sgu_wd
    q_blk, k_blk = q_off // hk, (q_off + dk) // hk
    v2_blk, r_blk = (q_off + 2 * dk) // hv, (q_off + 2 * dk + dv) // hv

    x_p2, x_s2 = x_prompt.reshape(mp, d), x_sample.reshape(nsamp, d)
    x = (x_p2, x_s2)
    xb = _stack_cast([x_p2, x_s2], BF16)
    w_in_t = jnp.swapaxes(w_in, 1, 2)

    pool_p, gla_p, pool_s, sgu_s = [], [], [], []
    gla_s = None
    state_pool_flat = state_pool.reshape(depth, nsamp, POOL_BUF * pool_wd)
    for i in range(depth):
        x, xb, (w_g,) = _ffn(x, xb, ffa_w1, ffa_w3, ffa_w2, i, ln_g[i, 0], ln_b[i, 0], alpha,
                             sides=[_gate_rows_cast(w_in_t, i, n_gate, d, 3)])

        b_in_i = b_in[i].reshape(1, -1)
        z = _in_proj(xb, w_in_t, b_in_i, i, n_main)
        logf = _gla_logf(xb, w_in_t, b_in_i, i, n_main,
                         jnp.pad(gla_wa2[i].astype(BF16), ((0, 128 - GLA_RANK), (0, 0))),
                         gla_ba[i].reshape(1, dk))

        pw = pool_w[i].astype(BF16)
        ps = pool_scale[i].reshape(1, pool_wd)
        ya, tail = _pool_prompt(z, pw, ps, batch=batch, seq=seq, m=m, width=pool_wd)
        ya, nbuf = _pool_sample(z, state_pool_flat, i, pw, ps, ya, row0=mp, nsamp=nsamp, width=pool_wd)
        pool_p.append(tail[:, POOL_HALO - POOL_BUF:])
        pool_s.append(nbuf.reshape(nsamp, POOL_BUF, pool_wd))

        lng = sgu_ln_g[i].reshape(1, sgu_wd)
        lnb = sgu_ln_b[i].reshape(1, sgu_wd)
        gd = sgu_wd // SGU_GROUPS
        yb = _sgu_prompt(z, lng, lnb, sgu_ws[i], sgu_bias[i].T, batch=batch, seq=seq, m=m, width=sgu_wd,
                         u_blk=u_blk, v_blk=v_blk)
        yb, vn = _sgu_sample(z, lng, lnb,
                             jnp.repeat(sgu_ws[i, :, 0, 0], gd).reshape(1, sgu_wd),
                             jnp.repeat(sgu_bias[i, :, 0], gd).reshape(1, sgu_wd),
                             yb, row0=mp, nsamp=nsamp, width=sgu_wd, u_blk=u_blk, v_blk=v_blk)
        sgu_s.append(vn.reshape(nsamp, 1, sgu_wd))

        ng = gla_norm_g[i].reshape(1, dv)
        blks = dict(hk=hk, hv=hv, q_blk=q_blk, k_blk=k_blk, v_blk=v2_blk, r_blk=r_blk)
        yc, s_p = _gla_prompt(z, logf, ng, batch=batch, seq=seq, m=m, hk=hk, hv=hv, q_off=q_off)
        yc, gla_s = _gla_sample(z, logf, ng, state_gla, i, yc, gla_s, row0=mp, nsamp=nsamp, **blks)
        gla_p.append(s_p)

        gt = _gate_tile(d)
        b_g = b_in_i[:, n_gate:].reshape(3, d // gt, gt).transpose(1, 0, 2).reshape(1, 3 * d)
        h = _up_gate(xb, ya, yb, yc, w_g, b_g, w_up_a, w_up_b, w_up_c, i)
        t = _matmul(h, w_o, i, tm_target=1664, tn_target=512, name="out_proj", residual=x, alpha=alpha)
        x, xb = _norm_stream(t, ln_g[i, 1], ln_b[i, 1])

        x, xb, _ = _ffn(x, xb, ffb_w1, ffb_w3, ffb_w2, i, ln_g[i, 2], ln_b[i, 2], alpha)

        pb = jnp.concatenate([p_prompt[i].reshape(mp, -1), p_sample[i].reshape(nsamp, -1)], axis=0).astype(BF16)
        t = _pe_embed(xb, pb, pe_gate_w, pe_w, i, x, alpha)
        if i + 1 < depth:
            x, xb = _norm_stream(t, ln_g[i, 3], ln_b[i, 3])
        else:
            ln = functools.partial(_res_ln, t, None, ln_g[i, 3], ln_b[i, 3], outputs=("f32",))
            (x_p,), (x_s,) = ln(rows=mp), ln(rows=nsamp, x_row0=mp)

    return (x_p.reshape(batch, seq, d), x_s.reshape(nsamp, 1, d),
            jnp.stack(pool_p), jnp.stack(gla_p).astype(state_gla.dtype),
            jnp.stack(pool_s), gla_s.astype(state_gla.dtype), jnp.stack(sgu_s))
```

```python
import functools
from typing import Callable, NamedTuple

import jax
import jax.numpy as jnp
from jax import lax
from jax.experimental import pallas as pl
from jax.experimental.pallas import tpu as pltpu

F32 = jnp.float32
BF16 = jnp.bfloat16

POOL_WINDOWS = (2, 4, 8, 16)
POOL_BUF = max(POOL_WINDOWS) - 1
POOL_HALO = 16
SGU_GROUPS = 8
SGU_CHUNK = 128
GLA_HEADS = 4
GLA_RANK = 16
GLA_TAU = 16.0
GLA_CHUNK = 64
EPS = 1e-5
PAST_LEN = 16384
SAMPLE_BLOCK = 8

LANES = 128
V7X_VMEM_LIMIT_BYTES = 60 * 1024 * 1024


def _tile(dim, target, align):
    best = None
    t = align
    while t <= min(dim, target):
        if dim % t == 0:
            best = t
        t += align
    return dim if best is None else best


def _params(*sem):
    return pltpu.CompilerParams(dimension_semantics=sem, vmem_limit_bytes=V7X_VMEM_LIMIT_BYTES)


def _silu(x):
    return x * jax.nn.sigmoid(x)


def _dot(a, b):
    return jnp.dot(a, b, preferred_element_type=F32)


def _dot_nt(a, b):
    return lax.dot_general(a, b, (((1,), (1,)), ((), ())), preferred_element_type=F32)


def _dot_tn(a, b):
    return lax.dot_general(a, b, (((0,), (0,)), ((), ())), preferred_element_type=F32)


def _round_bf16(x):
    return x.astype(BF16).astype(F32)


def _row_stats(v):
    mu = jnp.mean(v, -1, keepdims=True)
    d = v - mu
    return mu, lax.rsqrt(jnp.mean(d * d, -1, keepdims=True) + EPS)


def _apply_norm(v, mu, rstd, g, b):
    return (v - mu) * rstd * g + b


def _row_layer_norm(v, g, b):
    mu, rstd = _row_stats(v)
    return _apply_norm(v, mu, rstd, g, b)


def _residual(x, y, alpha, yscale):
    return alpha * x + (y if yscale is None else yscale * y)


class _Normed(NamedTuple):
    t: jax.Array
    mu: jax.Array
    rstd: jax.Array
    g: jax.Array
    b: jax.Array


def _stream_specs(x, tm, tn):
    tile = pl.BlockSpec((tm, tn), lambda i, j: (i, j))
    if not isinstance(x, _Normed):
        return [tile], [x]
    stat = pl.BlockSpec((tm, LANES), lambda i, j: (i, 0))
    vec = pl.BlockSpec((1, tn), lambda i, j: (0, j))
    return [tile, stat, stat, vec, vec], list(x)


def _stream_tile(refs):
    if len(refs) == 1:
        return refs[0][...]
    t_ref, mu_ref, rstd_ref, g_ref, b_ref = refs
    return _apply_norm(t_ref[...], mu_ref[:, 0:1], rstd_ref[:, 0:1], g_ref[...], b_ref[...])


def _w_spec(w, layer, k, tn):
    assert w.shape[-2] == k
    if w.ndim == 2:
        return pl.BlockSpec((k, tn), lambda i, j: (0, j))
    return pl.BlockSpec((None, k, tn), lambda i, j: (layer, 0, j))


def _resident_rows(tm, k):
    return pl.BlockSpec((tm, k), lambda i, j: (i, 0), pipeline_mode=pl.Buffered(1))


class _SideCast(NamedTuple):
    src: jax.Array
    n_slabs: int
    in_block: tuple
    in_index: Callable
    out_block: tuple
    out_index: Callable
    out_shape: jax.ShapeDtypeStruct


def _cast_kernel(w_ref, o_ref):
    o_ref[...] = w_ref[...].astype(o_ref.dtype)


def _run_side_cast(side):
    return pl.pallas_call(
        _cast_kernel,
        grid=(side.n_slabs,),
        in_specs=[pl.BlockSpec(side.in_block, side.in_index)],
        out_specs=pl.BlockSpec(side.out_block, side.out_index),
        out_shape=side.out_shape,
        compiler_params=_params("parallel"),
        name="cast_rows",
    )(side.src)


def _ffn_up_tiles(m, n):
    return _tile(m, 2080, 16), _tile(n, 256, 128)


def _ffn_up_steps(m, n):
    tm, tn = _ffn_up_tiles(m, n)
    return (m // tm) * (n // tn)


def _glu_kernel(x_ref, w1_ref, w3_ref, *refs, side_slabs, steps):
    n_side = len(side_slabs)
    x = x_ref[...]
    a = _dot(x, w1_ref[...].astype(BF16))
    b = _dot(x, w3_ref[...].astype(BF16))
    refs[n_side][...] = (_silu(a) * b).astype(refs[n_side].dtype)
    step = pl.program_id(0) * pl.num_programs(1) + pl.program_id(1)
    for n_slabs, src_ref, dst_ref in zip(side_slabs, refs[:n_side], refs[n_side + 1:]):
        def cast(src_ref=src_ref, dst_ref=dst_ref):
            dst_ref[...] = src_ref[...].astype(dst_ref.dtype)
        if n_slabs == steps:
            cast()
        else:
            pl.when(step < n_slabs)(cast)


def _ffn_up(xb, w1, w3, layer, sides=()):
    m, k = xb.shape
    n = w1.shape[-1]
    tm, tn = _ffn_up_tiles(m, n)
    nj = n // tn
    steps = _ffn_up_steps(m, n)
    riding = [s for s in sides if s.n_slabs <= steps]

    def slab_spec(block, index, n_slabs):
        return pl.BlockSpec(block, lambda i, j: index(jnp.minimum(i * nj + j, n_slabs - 1)))

    h, *casts = pl.pallas_call(
        functools.partial(_glu_kernel, side_slabs=tuple(s.n_slabs for s in riding), steps=steps),
        grid=(m // tm, nj),
        in_specs=[_resident_rows(tm, k),
                  _w_spec(w1, layer, k, tn),
                  _w_spec(w3, layer, k, tn),
                  *[slab_spec(s.in_block, s.in_index, s.n_slabs) for s in riding]],
        out_specs=[pl.BlockSpec((tm, tn), lambda i, j: (i, j)),
                   *[slab_spec(s.out_block, s.out_index, s.n_slabs) for s in riding]],
        out_shape=[jax.ShapeDtypeStruct((m, n), BF16), *[s.out_shape for s in riding]],
        compiler_params=_params("parallel", "arbitrary"),
        name="ffn_up",
    )(xb, w1, w3, *[s.src for s in riding])
    casts = iter(casts)
    return h, [next(casts) if s.n_slabs <= steps else _run_side_cast(s) for s in sides]


def _mm_kernel(x_ref, w_ref, o_ref):
    o_ref[...] = _dot(x_ref[...], w_ref[...].astype(BF16)).astype(o_ref.dtype)


def _mm_nt_bias_kernel(x_ref, wt_ref, b_ref, o_ref):
    o_ref[...] = (_dot_nt(x_ref[...], wt_ref[...].astype(BF16)) + b_ref[...]).astype(o_ref.dtype)


def _cast_into_kernel(x_ref, *refs):
    refs[-1][...] = x_ref[...].astype(refs[-1].dtype)


def _stack_cast(parts, dtype):
    d = parts[0].shape[1]
    m = sum(p.shape[0] for p in parts)
    out, row0 = None, 0
    for p in parts:
        rows = p.shape[0]
        tm = _tile(rows, 512, 16)
        assert row0 % tm == 0
        earlier = [] if out is None else [out]
        out = pl.pallas_call(
            _cast_into_kernel,
            grid=(rows // tm,),
            in_specs=[pl.BlockSpec((tm, d), lambda i: (i, 0))] + [pl.BlockSpec(memory_space=pl.ANY)] * len(earlier),
            out_specs=pl.BlockSpec((tm, d), lambda i, blk0=row0 // tm: (blk0 + i, 0)),
            out_shape=jax.ShapeDtypeStruct((m, d), dtype),
            input_output_aliases={1: 0} if earlier else {},
            compiler_params=_params("parallel"),
            name="stack_cast",
        )(p, *earlier)
        row0 += rows
    return out


def _gate_tile(d):
    return _tile(d, 256, 128)


def _gate_rows_cast(w, layer, row0, d, branches):
    _, _, k = w.shape
    tr = _gate_tile(d)
    nb = d // tr
    assert row0 % 8 == 0
    return _SideCast(
        src=w, n_slabs=branches * nb,
        in_block=(None, pl.Element(tr), pl.Element(k)),
        in_index=lambda s: (layer, pl.multiple_of(row0 + s * tr, 8), 0),
        out_block=(tr, k),
        out_index=lambda s: ((s % nb) * branches + s // nb, 0),
        out_shape=jax.ShapeDtypeStruct((branches * d, k), BF16))


def _layer_cast(w, layer, max_slabs, rows=None):
    n = w.shape[2]
    k = w.shape[1] if rows is None else rows
    slab = next((r for r in range(16, k + 1, 16) if k % r == 0 and k // r <= max_slabs), k)
    return _SideCast(
        src=w, n_slabs=k // slab,
        in_block=(None, slab, n), in_index=lambda s: (layer, s, 0),
        out_block=(slab, n), out_index=lambda s: (s, 0),
        out_shape=jax.ShapeDtypeStruct((k, n), BF16))


def _mm_res_kernel(x_ref, w_ref, *refs, alpha, yscale):
    *r_refs, o_ref = refs
    o_ref[...] = _residual(_stream_tile(r_refs), _dot(x_ref[...], w_ref[...].astype(BF16)), alpha, yscale)


def _matmul(xb, w, layer, *, tm_target, tn_target, name, residual=None, alpha=None, yscale=None):
    m, k = xb.shape
    n = w.shape[-1]
    tm = _tile(m, tm_target, 16)
    tn = _tile(n, tn_target, 128)
    tile = pl.BlockSpec((tm, tn), lambda i, j: (i, j))
    in_specs = [_resident_rows(tm, k), _w_spec(w, layer, k, tn)]
    args = [xb, w]
    kern = _mm_kernel
    if residual is not None:
        r_specs, r_args = _stream_specs(residual, tm, tn)
        in_specs += r_specs
        args += r_args
        kern = functools.partial(_mm_res_kernel, alpha=alpha, yscale=yscale)
    return pl.pallas_call(
        kern,
        grid=(m // tm, n // tn),
        in_specs=in_specs,
        out_specs=tile,
        out_shape=jax.ShapeDtypeStruct((m, n), F32),
        compiler_params=_params("parallel", "arbitrary"),
        name=name,
    )(*args)


def _in_proj(xb, w_t, bias):
    m, k = xb.shape
    n = w_t.shape[0]
    tm = _tile(m, 1664, 16)
    tn = _tile(n, 512, 128)
    return pl.pallas_call(
        _mm_nt_bias_kernel,
        grid=(m // tm, n // tn),
        in_specs=[_resident_rows(tm, k),
                  pl.BlockSpec((tn, k), lambda i, j: (j, 0)),
                  pl.BlockSpec((1, tn), lambda i, j: (0, j))],
        out_specs=pl.BlockSpec((tm, tn), lambda i, j: (i, j)),
        out_shape=jax.ShapeDtypeStruct((m, n), F32),
        compiler_params=_params("parallel", "arbitrary"),
        name="in_proj",
    )(xb, w_t, bias)


def _ln_kernel(*refs, alpha, yscale, n_src, n_alias, outputs):
    srcs, (g_ref, b_ref), outs = refs[:n_src], refs[n_src:n_src + 2], list(refs[n_src + 2 + n_alias:])
    t = srcs[0][...] if n_src == 1 else _residual(srcs[0][...], srcs[1][...], alpha, yscale)
    mu, rstd = _row_stats(t)
    o = _apply_norm(t, mu, rstd, g_ref[...], b_ref[...])
    for kind in outputs:
        if kind == "f32":
            outs.pop(0)[...] = o
        elif kind == "bf16":
            outs.pop(0)[...] = o.astype(BF16)
        else:
            for stat in (mu, rstd):
                ref = outs.pop(0)
                ref[...] = jnp.broadcast_to(stat, ref.shape)


def _res_ln(x, y, g, b, *, alpha=None, yscale=None, rows=None, x_row0=0, y_row0=0, out_row0=0, out_rows=None,
            outputs=("f32", "bf16"), into=None):
    d = x.shape[1]
    rows = x.shape[0] if rows is None else rows
    out_rows = rows if out_rows is None else out_rows
    tm = _tile(rows, 416, 16)
    assert x_row0 % tm == 0 and y_row0 % tm == 0 and out_row0 % tm == 0

    def at(row0, width=d):
        return pl.BlockSpec((tm, width), lambda i: (row0 // tm + i, 0))

    vec = pl.BlockSpec((1, d), lambda i: (0, 0))
    out_shape, out_specs = [], []
    for kind in outputs:
        if kind == "stats":
            out_shape += [jax.ShapeDtypeStruct((out_rows, LANES), F32)] * 2
            out_specs += [at(out_row0, LANES)] * 2
        else:
            out_shape.append(jax.ShapeDtypeStruct((out_rows, d), F32 if kind == "f32" else BF16))
            out_specs.append(at(out_row0))
    srcs = [(x, x_row0)] + ([] if y is None else [(y, y_row0)])
    in_specs = [at(r0) for _, r0 in srcs] + [vec, vec]
    args = [a for a, _ in srcs] + [g.reshape(1, d), b.reshape(1, d)]
    aliases = {}
    if into is not None:
        in_specs += [pl.BlockSpec(memory_space=pl.ANY)] * len(into)
        aliases = {len(args) + n: n for n in range(len(into))}
        args += list(into)
    return pl.pallas_call(
        functools.partial(_ln_kernel, alpha=alpha, yscale=yscale, n_src=len(srcs), n_alias=len(aliases),
                          outputs=outputs),
        grid=(rows // tm,),
        in_specs=in_specs,
        out_specs=out_specs,
        out_shape=out_shape,
        input_output_aliases=aliases,
        compiler_params=_params("parallel"),
        name="res_ln",
    )(*args)


def _log_sigmoid(x):
    return jnp.minimum(x, 0.0) - jnp.log1p(jnp.exp(-jnp.abs(x)))


def _logf_kernel(x_ref, wt_ref, bl_ref, wa_ref, ba_ref, o_ref):
    a_lr = _dot_nt(x_ref[...], wt_ref[...].astype(BF16)) + bl_ref[...]
    a_lr = jnp.where(lax.broadcasted_iota(jnp.int32, a_lr.shape, 1) < GLA_RANK, a_lr, 0.0)
    t = _dot(a_lr.astype(BF16), wa_ref[...]) + ba_ref[...]
    o_ref[...] = _log_sigmoid(t) / GLA_TAU


def _gla_logf(xb, w_in_t, b_in, layer, col0, wa2_pad, ba):
    m, k = xb.shape
    lane, dk = wa2_pad.shape
    assert col0 % lane == 0
    tm = _tile(m, 1040, 16)
    return pl.pallas_call(
        _logf_kernel,
        grid=(m // tm,),
        in_specs=[pl.BlockSpec((tm, k), lambda i: (i, 0)),
                  pl.BlockSpec((None, lane, k), lambda i: (layer, col0 // lane, 0)),
                  pl.BlockSpec((1, lane), lambda i: (0, col0 // lane)),
                  pl.BlockSpec((lane, dk), lambda i: (0, 0)),
                  pl.BlockSpec((1, dk), lambda i: (0, 0))],
        out_specs=pl.BlockSpec((tm, dk), lambda i: (i, 0)),
        out_shape=jax.ShapeDtypeStruct((m, dk), F32),
        compiler_params=_params("parallel"),
        name="gla_logf",
    )(xb, w_in_t, b_in, wa2_pad, ba)


def _upgate_kernel(x_ref, ya_ref, yb_ref, yc_ref, wg_ref, bg_ref, wa_ref, wb_ref, wc_ref, o_ref):
    tn = o_ref.shape[1]
    gates = jax.nn.sigmoid(_dot_nt(x_ref[...], wg_ref[...]) + bg_ref[...])
    acc = gates[:, :tn] * _dot(ya_ref[...], wa_ref[...].astype(BF16))
    acc = acc + gates[:, tn:2 * tn] * _dot(yb_ref[...], wb_ref[...].astype(BF16))
    acc = acc + gates[:, 2 * tn:] * _dot(yc_ref[...], wc_ref[...].astype(BF16))
    o_ref[...] = acc.astype(o_ref.dtype)


def _up_gate(xb, ya, yb, yc, w_g, b_g, w_up_a, w_up_b, w_up_c, layer):
    m, k = xb.shape
    d = w_up_a.shape[-1]
    tm = _tile(m, 1040, 16)
    tn = _gate_tile(d)

    def rows(width):
        return _resident_rows(tm, width)

    def up_cols(width):
        return pl.BlockSpec((None, width, tn), lambda i, j: (layer, 0, j))

    return pl.pallas_call(
        _upgate_kernel,
        grid=(m // tm, d // tn),
        in_specs=[rows(k), rows(ya.shape[1]), rows(yb.shape[1]), rows(yc.shape[1]),
                  pl.BlockSpec((3 * tn, k), lambda i, j: (j, 0)),
                  pl.BlockSpec((1, 3 * tn), lambda i, j: (0, j)),
                  up_cols(ya.shape[1]), up_cols(yb.shape[1]), up_cols(yc.shape[1])],
        out_specs=pl.BlockSpec((tm, tn), lambda i, j: (i, j)),
        out_shape=jax.ShapeDtypeStruct((m, d), BF16),
        compiler_params=_params("parallel", "arbitrary"),
        name="up_gate",
    )(xb, ya, yb, yc, w_g, b_g, w_up_a, w_up_b, w_up_c)


def _pe_kernel(x_ref, p_ref, wg_ref, wp_ref, *refs, alpha):
    *r_refs, o_ref = refs
    y = jax.nn.sigmoid(_dot(x_ref[...], wg_ref[...].astype(BF16))) * _dot(p_ref[...], wp_ref[...].astype(BF16))
    o_ref[...] = _residual(_stream_tile(r_refs), y, alpha, None)


def _pe_embed(xb, pb, w_gate, w_p, layer, residual, alpha):
    m, k = xb.shape
    d = w_gate.shape[-1]
    kp = pb.shape[1]
    tm = _tile(m, 1664, 16)
    tn = _tile(d, 512, 128)
    r_specs, r_args = _stream_specs(residual, tm, tn)
    return pl.pallas_call(
        functools.partial(_pe_kernel, alpha=alpha),
        grid=(m // tm, d // tn),
        in_specs=[_resident_rows(tm, k),
                  _resident_rows(tm, kp),
                  _w_spec(w_gate, layer, k, tn),
                  _w_spec(w_p, layer, kp, tn),
                  *r_specs],
        out_specs=pl.BlockSpec((tm, tn), lambda i, j: (i, j)),
        out_shape=jax.ShapeDtypeStruct((m, d), F32),
        compiler_params=_params("parallel", "arbitrary"),
        name="pe_embed",
    )(xb, pb, w_gate, w_p, *r_args)


def _pool_prompt_kernel(a_ref, pw_ref, ps_ref, o_ref, tail_ref, ext_ref, *, tt, gd):
    t = pl.program_id(1)

    @pl.when(t == pl.num_programs(1) - 1)
    def _():
        tail_ref[...] = a_ref[tt - POOL_HALO:tt, :]

    @pl.when(t == 0)
    def _():
        ext_ref[0:POOL_HALO, :] = jnp.zeros((POOL_HALO, ext_ref.shape[1]), F32)

    a = a_ref[...]
    ext_ref[POOL_HALO:POOL_HALO + tt, :] = a
    pos = t * tt + lax.broadcasted_iota(jnp.int32, (tt, 1), 0)
    for gi, w in enumerate(POOL_WINDOWS):
        sl = slice(gi * gd, (gi + 1) * gd)
        acc = a[:, sl]
        for s in range(1, w):
            acc = acc + ext_ref[POOL_HALO - s:POOL_HALO - s + tt, sl]
        cnt = jnp.minimum(pos + 1, w).astype(F32)
        d = acc / cnt - a[:, sl]
        y = _dot(d.astype(BF16), pw_ref[gi])
        o_ref[:, sl] = (y * ps_ref[:, sl]).astype(o_ref.dtype)
    ext_ref[0:POOL_HALO, :] = ext_ref[tt:tt + POOL_HALO, :]


def _pool_prompt(z, pool_w, pool_scale, *, batch, seq, m, width):
    tt = _tile(seq, 256, 16)
    nt = seq // tt
    gd = width // len(POOL_WINDOWS)
    return pl.pallas_call(
        functools.partial(_pool_prompt_kernel, tt=tt, gd=gd),
        grid=(batch, nt),
        in_specs=[pl.BlockSpec((tt, width), lambda b, t: (b * nt + t, 0)),
                  pl.BlockSpec(pool_w.shape, lambda b, t: (0, 0, 0)),
                  pl.BlockSpec((1, width), lambda b, t: (0, 0))],
        out_specs=[pl.BlockSpec((tt, width), lambda b, t: (b * nt + t, 0)),
                   pl.BlockSpec((None, POOL_HALO, width), lambda b, t: (b, 0, 0))],
        out_shape=[jax.ShapeDtypeStruct((m, width), BF16),
                   jax.ShapeDtypeStruct((batch, POOL_HALO, width), F32)],
        scratch_shapes=[pltpu.VMEM((tt + POOL_HALO, width), F32)],
        compiler_params=_params("parallel", "arbitrary"),
        name="pool_prompt",
    )(z, pool_w, pool_scale)


def _pool_sample_kernel(a_ref, buf_ref, pw_ref, ps_ref, ya_in_ref, o_ref, nbuf_ref, *, width, gd):
    del ya_in_ref
    a = a_ref[...]
    for gi, w in enumerate(POOL_WINDOWS):
        sl = slice(gi * gd, (gi + 1) * gd)
        acc = a[:, sl]
        for s in range(1, w):
            r = POOL_BUF - s
            acc = acc + buf_ref[:, r * width + gi * gd:r * width + (gi + 1) * gd]
        d = acc / float(min(PAST_LEN + 1, w)) - a[:, sl]
        y = _dot(d.astype(BF16), pw_ref[gi])
        o_ref[:, sl] = (y * ps_ref[:, sl]).astype(o_ref.dtype)
    nbuf_ref[:, 0:(POOL_BUF - 1) * width] = buf_ref[:, width:POOL_BUF * width]
    nbuf_ref[:, (POOL_BUF - 1) * width:POOL_BUF * width] = a


def _pool_sample(z, buf_flat, layer, pool_w, pool_scale, ya, *, row0, nsamp, width):
    ns = SAMPLE_BLOCK
    gd = width // len(POOL_WINDOWS)
    rb = row0 // ns
    ya, nbuf = pl.pallas_call(
        functools.partial(_pool_sample_kernel, width=width, gd=gd),
        grid=(nsamp // ns,),
        in_specs=[pl.BlockSpec((ns, width), lambda i: (rb + i, 0)),
                  pl.BlockSpec((None, ns, POOL_BUF * width), lambda i: (layer, i, 0)),
                  pl.BlockSpec(pool_w.shape, lambda i: (0, 0, 0)),
                  pl.BlockSpec((1, width), lambda i: (0, 0)),
                  pl.BlockSpec(memory_space=pl.ANY)],
        out_specs=[pl.BlockSpec((ns, width), lambda i: (rb + i, 0)),
                   pl.BlockSpec((ns, POOL_BUF * width), lambda i: (i, 0))],
        out_shape=[jax.ShapeDtypeStruct(ya.shape, ya.dtype),
                   jax.ShapeDtypeStruct(buf_flat.shape[1:], F32)],
        input_output_aliases={4: 0},
        compiler_params=_params("parallel"),
        name="pool_sample",
    )(z, buf_flat, pool_w, pool_scale, ya)
    return ya, nbuf


def _sgu_prompt_kernel(u_ref, v_ref, lng_ref, lnb_ref, ws_ref, biast_ref, o_ref, *, nchunk, gd):
    c = SGU_CHUNK
    tril = (lax.broadcasted_iota(jnp.int32, (c, c), 0) >= lax.broadcasted_iota(jnp.int32, (c, c), 1)).astype(F32)
    w_m = [(ws_ref[g] * tril).astype(BF16) for g in range(SGU_GROUPS)]
    for ci in range(nchunk):
        rows = slice(ci * c, (ci + 1) * c)
        vn = _row_layer_norm(v_ref[rows, :], lng_ref[...], lnb_ref[...]).astype(BF16)
        for g in range(SGU_GROUPS):
            sl = slice(g * gd, (g + 1) * gd)
            mixed = _dot(w_m[g], vn[:, sl]) + biast_ref[:, g:g + 1]
            o_ref[rows, sl] = (u_ref[rows, sl] * mixed).astype(o_ref.dtype)


def _sgu_prompt(z, ln_g, ln_b, ws, bias_t, *, batch, seq, m, width, u_blk, v_blk):
    assert seq % SGU_CHUNK == 0
    rows = _tile(seq, 512, SGU_CHUNK)
    nt = seq // rows
    gd = width // SGU_GROUPS
    return pl.pallas_call(
        functools.partial(_sgu_prompt_kernel, nchunk=rows // SGU_CHUNK, gd=gd),
        grid=(batch * nt,),
        in_specs=[pl.BlockSpec((rows, width), lambda i: (i, u_blk)),
                  pl.BlockSpec((rows, width), lambda i: (i, v_blk)),
                  pl.BlockSpec((1, width), lambda i: (0, 0)),
                  pl.BlockSpec((1, width), lambda i: (0, 0)),
                  pl.BlockSpec(ws.shape, lambda i: (0, 0, 0)),
                  pl.BlockSpec(bias_t.shape, lambda i: (0, 0))],
        out_specs=pl.BlockSpec((rows, width), lambda i: (i, 0)),
        out_shape=jax.ShapeDtypeStruct((m, width), BF16),
        compiler_params=_params("parallel"),
        name="sgu_prompt",
    )(z, z, ln_g, ln_b, ws, bias_t)


def _sgu_sample_kernel(u_ref, v_ref, lng_ref, lnb_ref, w0_ref, b0_ref, yb_in_ref, o_ref, vn_ref):
    del yb_in_ref
    vn = _row_layer_norm(v_ref[...], lng_ref[...], lnb_ref[...])
    vn_ref[...] = vn
    mixed = _round_bf16(w0_ref[...]) * _round_bf16(vn) + b0_ref[...]
    o_ref[...] = (u_ref[...] * mixed).astype(o_ref.dtype)


def _sgu_sample(z, ln_g, ln_b, w0, b0, yb, *, row0, nsamp, width, u_blk, v_blk):
    ns = SAMPLE_BLOCK
    rb = row0 // ns
    vec = pl.BlockSpec((1, width), lambda i: (0, 0))
    yb, vn = pl.pallas_call(
        _sgu_sample_kernel,
        grid=(nsamp // ns,),
        in_specs=[pl.BlockSpec((ns, width), lambda i: (rb + i, u_blk)),
                  pl.BlockSpec((ns, width), lambda i: (rb + i, v_blk)),
                  vec, vec, vec, vec,
                  pl.BlockSpec(memory_space=pl.ANY)],
        out_specs=[pl.BlockSpec((ns, width), lambda i: (rb + i, 0)),
                   pl.BlockSpec((ns, width), lambda i: (i, 0))],
        out_shape=[jax.ShapeDtypeStruct(yb.shape, yb.dtype),
                   jax.ShapeDtypeStruct((nsamp, width), F32)],
        input_output_aliases={6: 0},
        compiler_params=_params("parallel"),
        name="sgu_sample",
    )(z, z, ln_g, ln_b, w0, b0, yb)
    return yb, vn


def _cumsum_rows(x):
    rows = x.shape[0]
    row = lax.broadcasted_iota(jnp.int32, x.shape, 0)
    s = 1
    while s < rows:
        x = x + jnp.where(row >= s, pltpu.roll(x, s, 0), 0.0)
        s *= 2
    return x


def _gla_out(o, norm_g, r):
    o = o * lax.rsqrt(jnp.mean(o * o, -1, keepdims=True) + EPS) * norm_g
    return o * _silu(r)


def _gla_prompt_kernel(q_ref, k_ref, v01_ref, v23_ref, g_ref, r01_ref, r23_ref, ng_ref, o_ref, s_ref, st_ref, *,
                       nchunk, scale, hk, hv):
    t = pl.program_id(1)
    c = GLA_CHUNK

    @pl.when(t == 0)
    def _():
        st_ref[...] = jnp.zeros(st_ref.shape, F32)

    causal = (lax.broadcasted_iota(jnp.int32, (c, c), 0) >= lax.broadcasted_iota(jnp.int32, (c, c), 1)).astype(F32)
    v_refs = (v01_ref, v23_ref)
    r_refs = (r01_ref, r23_ref)
    for ci in range(nchunk):
        rows = slice(ci * c, (ci + 1) * c)
        for h in range(GLA_HEADS):
            kcols = slice(h * hk, (h + 1) * hk)
            vcols = slice((h % 2) * hv, (h % 2 + 1) * hv)
            ocols = slice(h * hv, (h + 1) * hv)
            b = _cumsum_rows(g_ref[rows, kcols])
            b_end = b[c - 1:c, :]
            k = k_ref[rows, kcols]
            q_in = ((q_ref[rows, kcols] * scale) * jnp.exp(b)).astype(BF16)
            k_in = (k * jnp.exp(-b)).astype(BF16)
            k_dec = (k * jnp.exp(b_end - b)).astype(BF16)
            vb = v_refs[h // 2][rows, vcols].astype(BF16)
            att = _dot_nt(q_in, k_in) * causal
            st = st_ref[h]
            o = _dot(att.astype(BF16), vb) + _dot_nt(q_in, st.astype(BF16))
            st_ref[h] = st * jnp.exp(b_end) + _dot_tn(vb, k_dec)
            o_ref[rows, ocols] = _gla_out(o, ng_ref[:, ocols], r_refs[h // 2][rows, vcols]).astype(o_ref.dtype)

    @pl.when(t == pl.num_programs(1) - 1)
    def _():
        for h in range(GLA_HEADS):
            s_ref[0, h] = st_ref[h].T


def _gla_prompt(z, logf, norm_g, *, batch, seq, m, hk, hv, q_off):
    assert seq % GLA_CHUNK == 0 and GLA_HEADS == 4
    rows = _tile(seq, 256, GLA_CHUNK)
    nt = seq // rows
    heads = GLA_HEADS
    dk, dv2 = heads * hk, 2 * hv
    v_off = q_off + 2 * dk
    assert q_off % dk == 0 and v_off % dv2 == 0

    def cols(width, off):
        return pl.BlockSpec((rows, width), lambda n, t: (n * nt + t, off // width))

    yc, s = pl.pallas_call(
        functools.partial(_gla_prompt_kernel, nchunk=rows // GLA_CHUNK, scale=hk ** -0.5, hk=hk, hv=hv),
        grid=(batch, nt),
        in_specs=[cols(dk, q_off), cols(dk, q_off + dk),
                  cols(dv2, v_off), cols(dv2, v_off + dv2),
                  cols(dk, 0),
                  cols(dv2, v_off + 2 * dv2), cols(dv2, v_off + 3 * dv2),
                  pl.BlockSpec((1, heads * hv), lambda n, t: (0, 0))],
        out_specs=[pl.BlockSpec((rows, heads * hv), lambda n, t: (n * nt + t, 0)),
                   pl.BlockSpec((1, heads, hk, hv), lambda n, t: (n, 0, 0, 0))],
        out_shape=[jax.ShapeDtypeStruct((m, heads * hv), BF16),
                   jax.ShapeDtypeStruct((batch, heads, hk, hv), F32)],
        scratch_shapes=[pltpu.VMEM((heads, hv, hk), F32)],
        compiler_params=_params("parallel", "arbitrary"),
        name="gla_prompt",
    )(z, z, z, z, logf, z, z, norm_g)
    return yc, s


def _gla_sample_kernel(*refs, scale):
    q_ref, k_ref, v_ref, g_ref, r_ref, ng_ref, s_ref, yc_in_ref = refs[:8]
    o_ref, so_ref = refs[-2:]
    del yc_in_ref
    ns = SAMPLE_BLOCK
    g = g_ref[...]
    e = jnp.exp(g)
    k = k_ref[...]
    q_in = _round_bf16((q_ref[...] * scale) * e)
    k_in = _round_bf16(k * jnp.exp(-g))
    k_dec = _round_bf16(k * jnp.exp(g - g))
    vb = _round_bf16(v_ref[...])
    att = jnp.sum(q_in * k_in, -1, keepdims=True)
    hk = g.shape[1]
    cols = jnp.concatenate([e, k_dec, q_in, jnp.zeros((128 - 3 * ns, hk), F32)], axis=0).T
    outs = []
    for j in range(ns):
        s = s_ref[0, j, 0]
        so_ref[0, j, 0] = s * cols[:, j:j + 1] + cols[:, ns + j:ns + j + 1] * vb[j:j + 1, :]
        outs.append(jnp.sum(cols[:, 2 * ns + j:2 * ns + j + 1] * _round_bf16(s), axis=0, keepdims=True))
    o = jnp.concatenate(outs, axis=0) + _round_bf16(att) * vb
    o_ref[...] = _gla_out(o, ng_ref[...], r_ref[...]).astype(o_ref.dtype)


def _gla_sample(z, logf, norm_g, state, layer, yc, prev_out, *, row0, nsamp, hk, hv, q_blk, k_blk, v_blk, r_blk):
    ns = SAMPLE_BLOCK
    rb = row0 // ns
    heads = GLA_HEADS
    st_spec = pl.BlockSpec((1, ns, 1, hk, hv), lambda i, h: (layer, i, h, 0, 0))
    in_specs = [pl.BlockSpec((ns, hk), lambda i, h: (rb + i, q_blk + h)),
                pl.BlockSpec((ns, hk), lambda i, h: (rb + i, k_blk + h)),
                pl.BlockSpec((ns, hv), lambda i, h: (rb + i, v_blk + h)),
                pl.BlockSpec((ns, hk), lambda i, h: (rb + i, h)),
                pl.BlockSpec((ns, hv), lambda i, h: (rb + i, r_blk + h)),
                pl.BlockSpec((1, hv), lambda i, h: (0, h)),
                st_spec,
                pl.BlockSpec(memory_space=pl.ANY)]
    args = [z, z, z, logf, z, norm_g, state, yc]
    aliases = {7: 0}
    if prev_out is not None:
        in_specs.append(pl.BlockSpec(memory_space=pl.ANY))
        args.append(prev_out)
        aliases[8] = 1
    yc, s_out = pl.pallas_call(
        functools.partial(_gla_sample_kernel, scale=hk ** -0.5),
        grid=(nsamp // ns, heads),
        in_specs=in_specs,
        out_specs=[pl.BlockSpec((ns, hv), lambda i, h: (rb + i, h)), st_spec],
        out_shape=[jax.ShapeDtypeStruct(yc.shape, yc.dtype),
                   jax.ShapeDtypeStruct(state.shape, F32)],
        input_output_aliases=aliases,
        compiler_params=_params("parallel", "arbitrary"),
        name="gla_sample",
    )(*args)
    return yc, s_out


def _norm_stream(t, g, b):
    g, b = g.reshape(1, -1), b.reshape(1, -1)
    xb, mu, rstd = _res_ln(t, None, g, b, outputs=("bf16", "stats"))
    return _Normed(t, mu, rstd, g, b), xb


def _ffn(x, xb, w1, w3, w2, layer, g, b, alpha, sides=()):
    w2_cast = _layer_cast(w2, layer, max_slabs=_ffn_up_steps(xb.shape[0], w1.shape[-1]))
    h, (w2b, *extra) = _ffn_up(xb, w1, w3, layer, [w2_cast, *sides])
    down = functools.partial(_matmul, h, w2b, None, tm_target=1040, tn_target=512, name="ffn_down")
    if not isinstance(x, tuple) or isinstance(x, _Normed):
        return (*_norm_stream(down(residual=x, alpha=alpha, yscale=0.5), g, b), extra)
    y = down()
    ln = functools.partial(_res_ln, g=g, b=b, alpha=alpha, yscale=0.5)
    x_p, x_s = x
    mp, m = x_p.shape[0], y.shape[0]
    outs = ln(x_p, y, out_rows=m)
    return (*ln(x_s, y, y_row0=mp, out_row0=mp, out_rows=m, into=outs), extra)


def kernel(x_prompt, x_sample, state_pool, state_gla, p_prompt, p_sample, ffa_w1, ffa_w3, ffa_w2, ffb_w1, ffb_w3, ffb_w2, ln_g, ln_b, w_in, b_in, pool_w, pool_scale, sgu_ln_g, sgu_ln_b, sgu_ws, sgu_bias, gla_wa2, gla_ba, gla_norm_g, w_up_a, w_up_b, w_up_c, w_o, pe_w, pe_gate_w):
    depth = ffa_w1.shape[0]
    batch, seq, d = x_prompt.shape
    nsamp, dec_seq, _ = x_sample.shape
    assert dec_seq == 1 and nsamp % SAMPLE_BLOCK == 0
    mp = batch * seq
    m = mp + nsamp
    assert mp % SAMPLE_BLOCK == 0
    alpha = (2.0 * depth) ** 0.25

    pool_wd = pool_w.shape[1] * pool_w.shape[2]
    sgu_wd = sgu_ln_g.shape[1]
    dk = gla_wa2.shape[2]
    dv = gla_norm_g.shape[1]
    hk, hv = dk // GLA_HEADS, dv // GLA_HEADS
    n_main = pool_wd + 2 * sgu_wd + 2 * dk + 2 * dv
    n_gate = n_main + GLA_RANK
    assert w_in.shape[2] == n_gate + 3 * d
    assert pool_wd == sgu_wd
    u_blk, v_blk = pool_wd // sgu_wd, pool_wd // sgu_wd + 1
    q_off = pool_wd + 2 * sgu_wd
    q_blk, k_blk = q_off // hk, (q_off + dk) // hk
    v2_blk, r_blk = (q_off + 2 * dk) // hv, (q_off + 2 * dk + dv) // hv

    x_p2, x_s2 = x_prompt.reshape(mp, d), x_sample.reshape(nsamp, d)
    x = (x_p2, x_s2)
    xb = _stack_cast([x_p2, x_s2], BF16)
    w_in_t = jnp.swapaxes(w_in, 1, 2)

    pool_p, gla_p, pool_s, sgu_s = [], [], [], []
    gla_s = None
    state_pool_flat = state_pool.reshape(depth, nsamp, POOL_BUF * pool_wd)
    for i in range(depth):
        steps = _ffn_up_steps(m, ffa_w1.shape[-1])
        x, xb, (w_g, w_main, w_o_b) = _ffn(
            x, xb, ffa_w1, ffa_w3, ffa_w2, i, ln_g[i, 0], ln_b[i, 0], alpha,
            sides=[_gate_rows_cast(w_in_t, i, n_gate, d, 3),
                   _layer_cast(w_in_t, i, steps - 1, rows=n_main),
                   _layer_cast(w_o, i, steps - 1)])

        b_in_i = b_in[i].reshape(1, -1)
        z = _in_proj(xb, w_main, b_in_i)
        logf = _gla_logf(xb, w_in_t, b_in_i, i, n_main,
                         jnp.pad(gla_wa2[i].astype(BF16), ((0, 128 - GLA_RANK), (0, 0))),
                         gla_ba[i].reshape(1, dk))

        pw = pool_w[i].astype(BF16)
        ps = pool_scale[i].reshape(1, pool_wd)
        ya, tail = _pool_prompt(z, pw, ps, batch=batch, seq=seq, m=m, width=pool_wd)
        ya, nbuf = _pool_sample(z, state_pool_flat, i, pw, ps, ya, row0=mp, nsamp=nsamp, width=pool_wd)
        pool_p.append(tail[:, POOL_HALO - POOL_BUF:])
        pool_s.append(nbuf.reshape(nsamp, POOL_BUF, pool_wd))

        lng = sgu_ln_g[i].reshape(1, sgu_wd)
        lnb = sgu_ln_b[i].reshape(1, sgu_wd)
        gd = sgu_wd // SGU_GROUPS
        yb = _sgu_prompt(z, lng, lnb, sgu_ws[i], sgu_bias[i].T, batch=batch, seq=seq, m=m, width=sgu_wd,
                         u_blk=u_blk, v_blk=v_blk)
        yb, vn = _sgu_sample(z, lng, lnb,
                             jnp.repeat(sgu_ws[i, :, 0, 0], gd).reshape(1, sgu_wd),
                             jnp.repeat(sgu_bias[i, :, 0], gd).reshape(1, sgu_wd),
                             yb, row0=mp, nsamp=nsamp, width=sgu_wd, u_blk=u_blk, v_blk=v_blk)
        sgu_s.append(vn.reshape(nsamp, 1, sgu_wd))

        ng = gla_norm_g[i].reshape(1, dv)
        blks = dict(hk=hk, hv=hv, q_blk=q_blk, k_blk=k_blk, v_blk=v2_blk, r_blk=r_blk)
        yc, s_p = _gla_prompt(z, logf, ng, batch=batch, seq=seq, m=m, hk=hk, hv=hv, q_off=q_off)
        yc, gla_s = _gla_sample(z, logf, ng, state_gla, i, yc, gla_s, row0=mp, nsamp=nsamp, **blks)
        gla_p.append(s_p)

        gt = _gate_tile(d)
        b_g = b_in_i[:, n_gate:].reshape(3, d // gt, gt).transpose(1, 0, 2).reshape(1, 3 * d)
        h = _up_gate(xb, ya, yb, yc, w_g, b_g, w_up_a, w_up_b, w_up_c, i)
        t = _matmul(h, w_o_b, None, tm_target=1664, tn_target=512, name="out_proj", residual=x, alpha=alpha)
        x, xb = _norm_stream(t, ln_g[i, 1], ln_b[i, 1])

        x, xb, (pe_gate_b,) = _ffn(x, xb, ffb_w1, ffb_w3, ffb_w2, i, ln_g[i, 2], ln_b[i, 2], alpha,
                                   sides=[_layer_cast(pe_gate_w, i, steps - 1)])

        pb = jnp.concatenate([p_prompt[i].reshape(mp, -1), p_sample[i].reshape(nsamp, -1)], axis=0).astype(BF16)
        t = _pe_embed(xb, pb, pe_gate_b, pe_w, i, x, alpha)
        if i + 1 < depth:
            x, xb = _norm_stream(t, ln_g[i, 3], ln_b[i, 3])
        else:
            ln = functools.partial(_res_ln, t, None, ln_g[i, 3], ln_b[i, 3], outputs=("f32",))
            (x_p,), (x_s,) = ln(rows=mp), ln(rows=nsamp, x_row0=mp)

    return (x_p.reshape(batch, seq, d), x_s.reshape(nsamp, 1, d),
            jnp.stack(pool_p), jnp.stack(gla_p).astype(state_gla.dtype),
            jnp.stack(pool_s), gla_s.astype(state_gla.dtype), jnp.stack(sgu_s))
```
